```python
import jax, jax.numpy as jnp
from jax import lax
import numpy as np

D_MODEL = 1024
BATCH = 1
SEQ = 16384
DEPTH = 4
DEC_BATCH = 32
DEC_SEQ = 16
PAST_LEN = 2048

CHUNK = 64
N_MIXERS = 3
N_A = (DEPTH + 2) // 3
N_B = (DEPTH + 1) // 3
N_C = DEPTH // 3
N_DENSE = (DEPTH + 1) // 2
N_MOE = DEPTH // 2
RMS_EPS = 1e-6
LN_EPS = 1e-5
CONV_W = 31
FOX_HEADS = 16
FOX_HD = D_MODEL // FOX_HEADS
Q_BLOCK = 128
FOX_FORGET_BIAS = 2.0
SGU_CHUNK = 128
SGU_GROUPS = 8
D_SGU = D_MODEL
SGU_GC = D_SGU // SGU_GROUPS
N_MEM = 256
XA_HEADS = 4
XA_HD = D_MODEL // XA_HEADS
D_FF = 2816
N_EXPERTS = 8
TOP_K = 2
D_EXP = 3584

kernel_name = "hybrid_streaming_encoder_step"


def rmsnorm(x, g):
    x32 = x.astype(jnp.float32)
    y = x32 * lax.rsqrt(jnp.mean(x32 * x32, axis=-1, keepdims=True) + RMS_EPS)
    return (y * g.astype(jnp.float32)).astype(x.dtype)


def layernorm(x, g, b):
    x32 = x.astype(jnp.float32)
    mu = jnp.mean(x32, axis=-1, keepdims=True)
    var = jnp.mean(jnp.square(x32 - mu), axis=-1, keepdims=True)
    y = (x32 - mu) * lax.rsqrt(var + LN_EPS)
    return (y * g.astype(jnp.float32) + b.astype(jnp.float32)).astype(x.dtype)


def conformer_conv(h, left_ctx, w_in, dw, dw_b, ln_g, ln_b, w_out):
    a, gate = jnp.split(jnp.einsum('btd,de->bte', h, w_in), 2, axis=-1)
    u = a * jax.nn.sigmoid(gate)
    ext = jnp.concatenate([left_ctx.astype(u.dtype), u], axis=1)
    c = lax.conv_general_dilated(ext, dw[:, None, :].astype(u.dtype), (1,), 'VALID',
                                 dimension_numbers=('NWC', 'WIO', 'NWC'),
                                 feature_group_count=D_MODEL) + dw_b
    c = jax.nn.silu(layernorm(c, ln_g, ln_b))
    return jnp.einsum('btd,de->bte', c, w_out), ext[:, -(CONV_W - 1):]


def fox_project(h, w_in, b_f):
    B, T, _ = h.shape
    proj = jnp.einsum('btd,de->bte', h, w_in)
    q = proj[..., :D_MODEL].reshape(B, T, FOX_HEADS, FOX_HD)
    k = proj[..., D_MODEL:2 * D_MODEL].reshape(B, T, FOX_HEADS, FOX_HD)
    v = proj[..., 2 * D_MODEL:3 * D_MODEL].reshape(B, T, FOX_HEADS, FOX_HD)
    logf = jax.nn.log_sigmoid(proj[..., 3 * D_MODEL:].astype(jnp.float32) + b_f.astype(jnp.float32))
    return q, k, v, logf


def fox_block(q, k, v, Fq, Fk, q_pos, k_pos):
    s = jnp.einsum('bqhd,bkhd->bhqk', q, k, preferred_element_type=jnp.float32) * (FOX_HD ** -0.5)
    s = s + (jnp.swapaxes(Fq, 1, 2)[..., :, None] - jnp.swapaxes(Fk, 1, 2)[..., None, :])
    s = jnp.where(k_pos[None, :] <= q_pos[:, None], s, -jnp.inf)
    p = jax.nn.softmax(s, axis=-1).astype(v.dtype)
    return jnp.einsum('bhqk,bkhd->bqhd', p, v)


def fox_prompt(q, k, v, logf):
    B, T, H, hd = q.shape
    F = jnp.cumsum(logf, axis=1)
    nb = T // Q_BLOCK
    qb = jnp.moveaxis(q.reshape(B, nb, Q_BLOCK, H, hd), 1, 0)
    Fb = jnp.moveaxis(F.reshape(B, nb, Q_BLOCK, H), 1, 0)
    pos = jnp.arange(T, dtype=jnp.int32)
    pb = pos.reshape(nb, Q_BLOCK)
    out = lax.map(lambda a: fox_block(a[0], k, v, a[1], F, a[2], pos), (qb, Fb, pb))
    return jnp.moveaxis(out, 0, 1).reshape(B, T, H * hd)


def fox_sample(q, k, v, logf, cache_k, cache_v, cache_logf):
    P = cache_k.shape[1]
    B, S = q.shape[:2]
    k_all = jnp.concatenate([cache_k.astype(k.dtype), k], axis=1)
    v_all = jnp.concatenate([cache_v.astype(v.dtype), v], axis=1)
    F_all = jnp.cumsum(jnp.concatenate([cache_logf.astype(jnp.float32), logf], axis=1), axis=1)
    out = fox_block(q, k_all, v_all, F_all[:, P:], F_all,
                    P + jnp.arange(S, dtype=jnp.int32), jnp.arange(P + S, dtype=jnp.int32))
    return out.reshape(B, S, D_MODEL)


def sgu_project(h, w_in, ln_g, ln_b):
    z = jax.nn.gelu(jnp.einsum('btd,de->bte', h, w_in), approximate=False)
    u, v = jnp.split(z, 2, axis=-1)
    return u, layernorm(v, ln_g, ln_b)


def sgu_masked(w_s):
    i = jnp.arange(SGU_CHUNK)
    mask = (i[None, :] // CHUNK) <= (i[:, None] // CHUNK)
    return jnp.where(mask[None], w_s, 0)


def sgu_prompt(h, w_in, ln_g, ln_b, w_s, b_s, w_out):
    u, v = sgu_project(h, w_in, ln_g, ln_b)
    B, T, _ = v.shape
    n = T // SGU_CHUNK
    vc = v.reshape(B, n, SGU_CHUNK, SGU_GROUPS, SGU_GC)
    mixed = jnp.einsum('gij,bnjgc->bnigc', sgu_masked(w_s), vc) \
        + jnp.swapaxes(b_s, 0, 1)[None, None, :, :, None]
    return jnp.einsum('bte,ed->btd', u * mixed.reshape(B, T, D_SGU), w_out)


def sgu_sample(h, w_in, ln_g, ln_b, w_s, b_s, w_out):
    u, v = sgu_project(h, w_in, ln_g, ln_b)
    B, S, _ = v.shape
    vc = v.reshape(B, S, SGU_GROUPS, SGU_GC)
    mixed = jnp.einsum('gij,bjgc->bigc', sgu_masked(w_s)[:, :S, :S], vc) \
        + jnp.swapaxes(b_s[:, :S], 0, 1)[None, :, :, None]
    return jnp.einsum('bte,ed->btd', u * mixed.reshape(B, S, D_SGU), w_out), v


def mem_kv(mem_n, w_kv):
    B, M, _ = mem_n.shape
    kv = jnp.einsum('bmd,de->bme', mem_n, w_kv)
    return (kv[..., :D_MODEL].reshape(B, M, XA_HEADS, XA_HD),
            kv[..., D_MODEL:].reshape(B, M, XA_HEADS, XA_HD))


def cross_attend(h, k, v, w_q, w_o):
    B, T, _ = h.shape
    q = jnp.einsum('btd,de->bte', h, w_q).reshape(B, T, XA_HEADS, XA_HD)
    s = jnp.einsum('bthd,bmhd->bhtm', q, k.astype(q.dtype), preferred_element_type=jnp.float32) * (XA_HD ** -0.5)
    p = jax.nn.softmax(s, axis=-1).astype(h.dtype)
    o = jnp.einsum('bhtm,bmhd->bthd', p, v.astype(h.dtype)).reshape(B, T, D_MODEL)
    return jnp.einsum('btd,de->bte', o, w_o)


def swiglu(h, w_gu, w_down):
    g, u = jnp.split(jnp.einsum('btd,df->btf', h, w_gu), 2, axis=-1)
    return jnp.einsum('btf,fd->btd', jax.nn.silu(g) * u, w_down)


def moe_swiglu(h, router, w_gu, w_down):
    logits = jnp.einsum('btd,de->bte', h, router, preferred_element_type=jnp.float32)
    top_v, top_i = lax.top_k(logits, TOP_K)
    w = jax.nn.softmax(top_v, axis=-1)
    gates = jnp.sum(jax.nn.one_hot(top_i, N_EXPERTS, dtype=jnp.float32) * w[..., None], axis=-2)
    y = jnp.zeros_like(h)
    for e in range(N_EXPERTS):
        y = y + gates[..., e:e + 1].astype(h.dtype) * swiglu(h, w_gu[e], w_down[e])
    return y


def setup_inputs(seed: int = 0) -> dict:
    key = jax.random.key(seed)
    keys = iter(jax.random.split(key, 48))

    def nrm(shape, scale=1.0):
        return scale * jax.random.normal(next(keys), shape, jnp.float32)

    def gain(shape):
        return 1.0 + nrm(shape, 0.02)

    D = D_MODEL
    return {
        'x_prompt': nrm((BATCH, SEQ, D)),
        'x_sample': nrm((DEC_BATCH, DEC_SEQ, D)),
        'mem_prompt': nrm((BATCH, N_MEM, D)),
        'cache_mem_k': nrm((DEPTH, DEC_BATCH, N_MEM, XA_HEADS, XA_HD)),
        'cache_mem_v': nrm((DEPTH, DEC_BATCH, N_MEM, XA_HEADS, XA_HD)),
        'state_conv': nrm((N_A, DEC_BATCH, CONV_W - 1, D), 0.5),
        'cache_fox_k': nrm((N_B, DEC_BATCH, PAST_LEN, FOX_HEADS, FOX_HD)),
        'cache_fox_v': nrm((N_B, DEC_BATCH, PAST_LEN, FOX_HEADS, FOX_HD)),
        'cache_fox_logf': jax.nn.log_sigmoid(FOX_FORGET_BIAS + nrm((N_B, DEC_BATCH, PAST_LEN, FOX_HEADS), 0.5)),
        'norm_mix': gain((DEPTH, D)),
        'norm_mem_q': gain((DEPTH, D)),
        'norm_ffn': gain((DEPTH, D)),
        'norm_mem': gain((D,)),
        'norm_final': gain((D,)),
        'conv_w_in': nrm((N_A, D, 2 * D), D ** -0.5),
        'conv_dw': nrm((N_A, CONV_W, D), CONV_W ** -0.5),
        'conv_dw_b': nrm((N_A, D), 0.02),
        'conv_ln_g': gain((N_A, D)),
        'conv_ln_b': nrm((N_A, D), 0.02),
        'conv_w_out': nrm((N_A, D, D), D ** -0.5),
        'fox_w_in': nrm((N_B, D, 3 * D + FOX_HEADS), D ** -0.5),
        'fox_b_f': FOX_FORGET_BIAS + nrm((N_B, FOX_HEADS), 0.1),
        'fox_w_out': nrm((N_B, D, D), D ** -0.5),
        'sgu_w_in': nrm((N_C, D, 2 * D_SGU), D ** -0.5),
        'sgu_ln_g': gain((N_C, D_SGU)),
        'sgu_ln_b': nrm((N_C, D_SGU), 0.02),
        'sgu_w_s': nrm((N_C, SGU_GROUPS, SGU_CHUNK, SGU_CHUNK), SGU_CHUNK ** -0.5),
        'sgu_b_s': gain((N_C, SGU_GROUPS, SGU_CHUNK)),
        'sgu_w_out': nrm((N_C, D_SGU, D), D_SGU ** -0.5),
        'xa_w_q': nrm((DEPTH, D, D), D ** -0.5),
        'xa_w_kv': nrm((DEPTH, D, 2 * D), D ** -0.5),
        'xa_w_o': nrm((DEPTH, D, D), D ** -0.5),
        'ffn_w_gu': nrm((N_DENSE, D, 2 * D_FF), D ** -0.5),
        'ffn_w_down': nrm((N_DENSE, D_FF, D), D_FF ** -0.5),
        'moe_router': nrm((N_MOE, D, N_EXPERTS), D ** -0.5),
        'moe_w_gu': nrm((N_MOE, N_EXPERTS, D, 2 * D_EXP), D ** -0.5),
        'moe_w_down': nrm((N_MOE, N_EXPERTS, D_EXP, D), D_EXP ** -0.5),
    }


def reference(x_prompt, x_sample, mem_prompt, cache_mem_k, cache_mem_v, state_conv,
              cache_fox_k, cache_fox_v, cache_fox_logf,
              norm_mix, norm_mem_q, norm_ffn, norm_mem, norm_final,
              conv_w_in, conv_dw, conv_dw_b, conv_ln_g, conv_ln_b, conv_w_out,
              fox_w_in, fox_b_f, fox_w_out,
              sgu_w_in, sgu_ln_g, sgu_ln_b, sgu_w_s, sgu_b_s, sgu_w_out,
              xa_w_q, xa_w_kv, xa_w_o,
              ffn_w_gu, ffn_w_down, moe_router, moe_w_gu, moe_w_down):
    xp, xs = x_prompt, x_sample
    mem_n = rmsnorm(mem_prompt, norm_mem)
    mem_k_p, mem_v_p = [], []
    conv_p, conv_s = [], []
    fk_p, fv_p, fl_p, fk_s, fv_s, fl_s = [], [], [], [], [], []
    sgu_v_s = []
    for i in range(DEPTH):
        kind = i % N_MIXERS
        j = i // N_MIXERS
        hp = rmsnorm(xp, norm_mix[i])
        hs = rmsnorm(xs, norm_mix[i])
        if kind == 0:
            cw = (conv_w_in[j], conv_dw[j], conv_dw_b[j], conv_ln_g[j], conv_ln_b[j], conv_w_out[j])
            zero_ctx = jnp.zeros((xp.shape[0], CONV_W - 1, D_MODEL), xp.dtype)
            mp, bp = conformer_conv(hp, zero_ctx, *cw)
            ms, bs = conformer_conv(hs, state_conv[j], *cw)
            conv_p.append(bp)
            conv_s.append(bs)
        elif kind == 1:
            qp, kp, vp, lfp = fox_project(hp, fox_w_in[j], fox_b_f[j])
            mp = jnp.einsum('btd,de->bte', fox_prompt(qp, kp, vp, lfp), fox_w_out[j])
            qs, ks, vs, lfs = fox_project(hs, fox_w_in[j], fox_b_f[j])
            ms = jnp.einsum('btd,de->bte',
                            fox_sample(qs, ks, vs, lfs, cache_fox_k[j], cache_fox_v[j], cache_fox_logf[j]),
                            fox_w_out[j])
            fk_p.append(kp); fv_p.append(vp); fl_p.append(lfp)
            fk_s.append(ks); fv_s.append(vs); fl_s.append(lfs)
        else:
            sw = (sgu_w_in[j], sgu_ln_g[j], sgu_ln_b[j], sgu_w_s[j], sgu_b_s[j], sgu_w_out[j])
            mp = sgu_prompt(hp, *sw)
            ms, v_new = sgu_sample(hs, *sw)
            sgu_v_s.append(v_new)
        xp = xp + mp
        xs = xs + ms
        kmp, vmp = mem_kv(mem_n, xa_w_kv[i])
        mem_k_p.append(kmp)
        mem_v_p.append(vmp)
        xp = xp + cross_attend(rmsnorm(xp, norm_mem_q[i]), kmp, vmp, xa_w_q[i], xa_w_o[i])
        xs = xs + cross_attend(rmsnorm(xs, norm_mem_q[i]), cache_mem_k[i], cache_mem_v[i], xa_w_q[i], xa_w_o[i])
        hp = rmsnorm(xp, norm_ffn[i])
        hs = rmsnorm(xs, norm_ffn[i])
        c = i // 2
        if i % 2 == 0:
            xp = xp + swiglu(hp, ffn_w_gu[c], ffn_w_down[c])
            xs = xs + swiglu(hs, ffn_w_gu[c], ffn_w_down[c])
        else:
            xp = xp + moe_swiglu(hp, moe_router[c], moe_w_gu[c], moe_w_down[c])
            xs = xs + moe_swiglu(hs, moe_router[c], moe_w_gu[c], moe_w_down[c])
    y_prompt = rmsnorm(xp, norm_final)
    y_sample = rmsnorm(xs, norm_final)
    mem_k_prompt = jnp.stack(mem_k_p)
    mem_v_prompt = jnp.stack(mem_v_p)
    conv_state_prompt = jnp.stack(conv_p)
    conv_state_sample = jnp.stack(conv_s)
    fox_k_prompt = jnp.stack(fk_p)
    fox_v_prompt = jnp.stack(fv_p)
    fox_logf_prompt = jnp.stack(fl_p)
    fox_k_sample = jnp.stack(fk_s)
    fox_v_sample = jnp.stack(fv_s)
    fox_logf_sample = jnp.stack(fl_s)
    sgu_v_sample = jnp.stack(sgu_v_s)
    return (y_prompt, y_sample, mem_k_prompt, mem_v_prompt, conv_state_prompt, conv_state_sample,
            fox_k_prompt, fox_v_prompt, fox_logf_prompt, fox_k_sample, fox_v_sample, fox_logf_sample,
            sgu_v_sample)
```

```python
import functools

import numpy as np
import jax
import jax.numpy as jnp
from jax import lax
from jax.experimental import pallas as pl
from jax.experimental.pallas import tpu as pltpu

F32 = jnp.float32
BF16 = jnp.bfloat16
RMS_EPS = 1e-6
LN_EPS = 1e-5
SGU_CHUNK = 128
SGU_BLOCK = 64
TOP_K = 2
LANES = 128
NEG = -1e30
VMEM_LIMIT = 52 * 1024 * 1024


def _params(n_axes, vmem=VMEM_LIMIT):
    return pltpu.CompilerParams(dimension_semantics=("arbitrary",) * n_axes, vmem_limit_bytes=vmem)


def _dot(a, b):
    return jnp.dot(a, b, preferred_element_type=F32)


def _dot_nt(a, b):
    return lax.dot_general(a, b, (((1,), (1,)), ((), ())), preferred_element_type=F32)


def _rms(x, g):
    return x * lax.rsqrt(jnp.mean(x * x, axis=-1, keepdims=True) + RMS_EPS) * g


def _ln(x, g, b):
    mu = jnp.mean(x, axis=-1, keepdims=True)
    xc = x - mu
    var = jnp.mean(xc * xc, axis=-1, keepdims=True)
    return xc * lax.rsqrt(var + LN_EPS) * g + b


def _silu(x):
    return x * jax.nn.sigmoid(x)


def _log_sigmoid(x):
    return jnp.minimum(x, 0.0) - jnp.log(1.0 + jnp.exp(-jnp.abs(x)))


def _split3(x):
    a = x.astype(BF16)
    r = x - a.astype(F32)
    b = r.astype(BF16)
    c = (r - b.astype(F32)).astype(BF16)
    return a, b, c


def _split2(x):
    a = x.astype(BF16)
    return a, (x - a.astype(F32)).astype(BF16)


def _dot_hi(x32, w_hi, w_lo):
    xh, xl = _split2(x32)
    return _dot(xh, w_hi) + _dot(xl, w_hi) + _dot(xh, w_lo)


def _full_spec(a):
    nd = a.ndim
    return pl.BlockSpec(a.shape, lambda *_, nd=nd: (0,) * nd)


def _row_call(body, *, grid_tiles, tm, first_tile, row_ins, full_ins, outs, alias_x=True,
              scratch=(), name=None):
    in_specs = [pl.BlockSpec((tm, a.shape[1]), lambda i, ft=first_tile: (i + ft, 0)) for a in row_ins]
    in_specs += [_full_spec(a) for a in full_ins]
    out_shape, out_specs = [], []
    for rows, cols, dt, off in outs:
        out_shape.append(jax.ShapeDtypeStruct((rows, cols), dt))
        out_specs.append(pl.BlockSpec((tm, cols), lambda i, off=off: (i + off, 0)))
    return pl.pallas_call(
        body, grid=(grid_tiles,), in_specs=in_specs, out_specs=out_specs, out_shape=out_shape,
        input_output_aliases={0: 0} if alias_x else {}, scratch_shapes=list(scratch),
        compiler_params=_params(1), name=name)(*row_ins, *full_ins)


def _mem_kv_body(mem_ref, g_ref, w_ref, k_ref, v_ref):
    d = mem_ref.shape[1]
    mn = _rms(mem_ref[...], g_ref[...]).astype(BF16)
    kv = _dot(mn, w_ref[0])
    k_ref[0] = kv[:, :d]
    v_ref[0] = kv[:, d:]


def _mem_kv(mem, g, w_kv):
    depth, d, _ = w_kv.shape
    n_mem = mem.shape[0]
    return pl.pallas_call(
        _mem_kv_body, grid=(depth,),
        in_specs=[_full_spec(mem), _full_spec(g), pl.BlockSpec((1, d, 2 * d), lambda i: (i, 0, 0))],
        out_specs=[pl.BlockSpec((1, n_mem, d), lambda i: (i, 0, 0))] * 2,
        out_shape=[jax.ShapeDtypeStruct((depth, n_mem, d), F32)] * 2,
        compiler_params=_params(1), name="mem_kv")(mem, g, w_kv)


def _conv_in_body(x_ref, g_ref, w_ref, u_ref):
    d = u_ref.shape[1]
    hn = _rms(x_ref[...], g_ref[...]).astype(BF16)
    z = _dot(hn, w_ref[...])
    u_ref[...] = z[:, :d] * jax.nn.sigmoid(z[:, d:])


def _conv_tail(c, lg, lb, wo):
    c = _silu(_ln(c, lg, lb))
    return _dot(c.astype(BF16), wo)


def _conv_prompt_body(x_ref, u_ref, halo_ref, dw_ref, dwb_ref, lg_ref, lb_ref, wo_ref, o_ref,
                      ext_ref, sh_ref, c_ref, *, taps, halo, rc):
    tm, d = u_ref.shape
    i = pl.program_id(0)
    ext_ref[0:halo, :] = jnp.where(i == 0, 0.0, halo_ref[...])
    ext_ref[halo:halo + tm, :] = u_ref[...]
    ext_ref[halo + tm:, :] = jnp.zeros((ext_ref.shape[0] - halo - tm, d), F32)
    off = halo - (taps - 1)
    n_sh = sh_ref.shape[1]
    for b in range(8):
        sh_ref[b] = ext_ref[pl.ds(off + b, n_sh), :]

    def chunk(r, carry):
        r0 = pl.multiple_of(r * rc, rc)
        acc = jnp.zeros((rc, d), F32) + dwb_ref[...]
        for k in range(taps):
            a, b = divmod(k, 8)
            acc = acc + sh_ref[b, pl.ds(r0 + 8 * a, rc), :] * dw_ref[k:k + 1, :]
        c_ref[pl.ds(r0, rc), :] = acc
        return carry

    lax.fori_loop(0, tm // rc, chunk, 0)
    o_ref[...] = x_ref[...] + _conv_tail(c_ref[...], lg_ref[...], lb_ref[...], wo_ref[...])


def _conv_sample_body(x_ref, ext_ref, dw_ref, dwb_ref, lg_ref, lb_ref, wo_ref, o_ref, c_ref, *, taps):
    bb, ext_len, d = ext_ref.shape
    s = ext_len - (taps - 1)
    for b in range(bb):
        acc = jnp.zeros((s, d), F32) + dwb_ref[...]
        for k in range(taps):
            acc = acc + ext_ref[b, k:k + s, :] * dw_ref[k:k + 1, :]
        c_ref[b * s:(b + 1) * s, :] = acc
    o_ref[...] = x_ref[...] + _conv_tail(c_ref[...], lg_ref[...], lb_ref[...], wo_ref[...])


def _fox_qkv(x_ref, g_ref, wqkv_ref, wfh_ref, wfl_ref, bf_ref):
    d = x_ref.shape[1]
    hn32 = _rms(x_ref[...], g_ref[...])
    qkv = _dot(hn32.astype(BF16), wqkv_ref[...])
    lg = _dot_hi(hn32, wfh_ref[...], wfl_ref[...]) + bf_ref[...]
    return hn32, qkv[:, :d], qkv[:, d:2 * d], qkv[:, 2 * d:], _log_sigmoid(lg)


def _fox_proj_sample_body(x_ref, g_ref, wqkv_ref, wfh_ref, wfl_ref, bf_ref,
                          q_ref, k_ref, v_ref, lf_ref):
    _, q, k, v, logf = _fox_qkv(x_ref, g_ref, wqkv_ref, wfh_ref, wfl_ref, bf_ref)
    q_ref[...] = q.astype(BF16)
    k_ref[...] = k
    v_ref[...] = v
    lf_ref[...] = logf[:, :lf_ref.shape[1]]


def _fox_proj_prompt_body(x_ref, g_ref, wqkv_ref, wfh_ref, wfl_ref, bf_ref, wfth_ref, wftl_ref, bfc_ref,
                          q_ref, k_ref, v_ref, kb_ref, vb_ref, lf_ref, f_ref, ft_ref,
                          crow_ref, ccol_ref):
    tm = x_ref.shape[0]
    heads = lf_ref.shape[1]
    i = pl.program_id(0)

    @pl.when(i == 0)
    def _():
        crow_ref[...] = jnp.zeros_like(crow_ref)
        ccol_ref[...] = jnp.zeros_like(ccol_ref)

    hn32, q, k, v, logf = _fox_qkv(x_ref, g_ref, wqkv_ref, wfh_ref, wfl_ref, bf_ref)
    q_ref[...] = q.astype(BF16)
    k_ref[...] = k
    v_ref[...] = v
    kb_ref[...] = k.astype(BF16)
    vb_ref[...] = v.astype(BF16)
    lf_ref[...] = logf[:, :heads]

    lane = lax.broadcasted_iota(jnp.int32, (tm, LANES), 1)
    logf = jnp.where(lane < heads, logf, 0.0)
    hh, hl = _split2(hn32)
    lgt = _dot_nt(wfth_ref[...], hh) + _dot_nt(wfth_ref[...], hl) + _dot_nt(wftl_ref[...], hh) + bfc_ref[...]
    sub = lax.broadcasted_iota(jnp.int32, (LANES, tm), 0)
    logft = jnp.where(sub < heads, _log_sigmoid(lgt), 0.0)

    r = lax.broadcasted_iota(jnp.int32, (tm, tm), 0)
    c = lax.broadcasted_iota(jnp.int32, (tm, tm), 1)
    lower = jnp.where(c <= r, 1.0, 0.0).astype(BF16)
    upper = jnp.where(r <= c, 1.0, 0.0).astype(BF16)
    ones_r = jnp.ones((8, tm), BF16)
    ones_c = jnp.ones((tm, LANES), BF16)
    a1, a2, a3 = _split3(logf)
    b1, b2, b3 = _split3(logft)
    f_ref[...] = _dot(lower, a1) + _dot(lower, a2) + _dot(lower, a3) + crow_ref[0:1, :]
    ft = _dot(b1, upper) + _dot(b2, upper) + _dot(b3, upper)
    ft_ref[...] = ft + jnp.tile(ccol_ref[...], (1, tm // LANES))
    crow_ref[...] = crow_ref[...] + _dot(ones_r, a1) + _dot(ones_r, a2) + _dot(ones_r, a3)
    ccol_ref[...] = ccol_ref[...] + _dot(b1, ones_c) + _dot(b2, ones_c) + _dot(b3, ones_c)


def _fox_proj_prompt(x_all, weights, *, t_p, tm, heads):
    d = x_all.shape[1]
    tile = lambda cols: pl.BlockSpec((tm, cols), lambda i: (i, 0))
    sds = lambda cols, dt: jax.ShapeDtypeStruct((t_p, cols), dt)
    return pl.pallas_call(
        _fox_proj_prompt_body, grid=(t_p // tm,),
        in_specs=[tile(d)] + [_full_spec(w) for w in weights],
        out_specs=[tile(d)] * 5 + [tile(heads), tile(LANES), pl.BlockSpec((LANES, tm), lambda i: (0, i))],
        out_shape=[sds(d, BF16), sds(d, F32), sds(d, F32), sds(d, BF16), sds(d, BF16), sds(heads, F32),
                   sds(LANES, F32), jax.ShapeDtypeStruct((LANES, t_p), F32)],
        scratch_shapes=[pltpu.VMEM((8, LANES), F32), pltpu.VMEM((LANES, LANES), F32)],
        compiler_params=_params(1), name="fox_proj_prompt")(x_all, *weights)


def _fox_attn_prompt_body(qi_ref, kj_ref, first_ref, last_ref,
                          x_ref, q_ref, k_ref, v_ref, f_ref, ft_ref, wo_ref, o_ref,
                          acc_ref, m_ref, ob_ref, *, heads, hd):
    tq = q_ref.shape[0]
    tk = k_ref.shape[0]
    s_id = pl.program_id(0)
    qi = qi_ref[s_id]
    kj = kj_ref[s_id]

    @pl.when(first_ref[s_id] == 1)
    def _():
        acc_ref[...] = jnp.zeros_like(acc_ref)
        m_ref[...] = jnp.full_like(m_ref, NEG)

    lane_k = lax.broadcasted_iota(jnp.int32, (tk, LANES), 1)
    row = lax.broadcasted_iota(jnp.int32, (tq, tk), 0) + qi * tq
    col = lax.broadcasted_iota(jnp.int32, (tq, tk), 1) + kj * tk
    causal = col <= row
    keep_lo = jnp.where(lane_k < hd, 1.0, 0.0).astype(BF16)
    keep_hi = jnp.where(lane_k >= hd, 1.0, 0.0).astype(BF16)
    ones_lo = jnp.where(lane_k == hd, 1.0, 0.0).astype(BF16)
    ones_hi = jnp.where(lane_k == 0, 1.0, 0.0).astype(BF16)
    for p in range(heads // 2):
        sl = slice(p * LANES, (p + 1) * LANES)
        qp = q_ref[:, sl]
        kp = k_ref[:, sl]
        vp = v_ref[:, sl]
        variants = ((2 * p, kp * keep_lo, vp * keep_lo + ones_lo),
                    (2 * p + 1, kp * keep_hi, vp * keep_hi + ones_hi))
        for h, kk, vv in variants:
            s = _dot_nt(qp, kk)
            fk = ft_ref[h:h + 1, :] - f_ref[0:1, h:h + 1]
            t = jnp.where(causal, s - fk, NEG)
            m_prev = m_ref[h]
            m_new = jnp.maximum(m_prev, jnp.max(t, axis=1, keepdims=True))
            alpha = jnp.exp(m_prev - m_new)
            pexp = jnp.exp(t - m_new).astype(BF16)
            acc_ref[h] = alpha * acc_ref[h] + _dot(pexp, vv)
            m_ref[h] = m_new

    @pl.when(last_ref[s_id] == 1)
    def _():
        lane_q = lax.broadcasted_iota(jnp.int32, (tq, LANES), 1)
        for p in range(heads // 2):
            ae = acc_ref[2 * p]
            ao = acc_ref[2 * p + 1]
            o = jnp.where(lane_q < hd, ae / ae[:, hd:hd + 1], ao / ao[:, 0:1])
            ob_ref[:, p * LANES:(p + 1) * LANES] = o.astype(BF16)
        o_ref[...] = x_ref[...] + _dot(ob_ref[...], wo_ref[...])


def _fox_attn_prompt(x_all, q, kb, vb, f_nat, f_t, w_out, *, t_p, tq, heads):
    d = x_all.shape[1]
    hd = d // heads
    nq = t_p // tq
    qi = np.concatenate([np.full(i + 1, i) for i in range(nq)]).astype(np.int32)
    kj = np.concatenate([np.arange(i + 1) for i in range(nq)]).astype(np.int32)
    first = (kj == 0).astype(np.int32)
    last = (kj == qi).astype(np.int32)
    n_steps = len(qi)
    grid_spec = pltpu.PrefetchScalarGridSpec(
        num_scalar_prefetch=4, grid=(n_steps,),
        in_specs=[
            pl.BlockSpec((tq, d), lambda s, qi, kj, fi, la: (qi[s], 0)),
            pl.BlockSpec((tq, d), lambda s, qi, kj, fi, la: (qi[s], 0)),
            pl.BlockSpec((tq, d), lambda s, qi, kj, fi, la: (kj[s], 0)),
            pl.BlockSpec((tq, d), lambda s, qi, kj, fi, la: (kj[s], 0)),
            pl.BlockSpec((tq, LANES), lambda s, qi, kj, fi, la: (qi[s], 0)),
            pl.BlockSpec((heads, tq), lambda s, qi, kj, fi, la: (0, kj[s])),
            pl.BlockSpec((d, d), lambda s, qi, kj, fi, la: (0, 0)),
        ],
        out_specs=pl.BlockSpec((tq, d), lambda s, qi, kj, fi, la: (qi[s], 0)),
        scratch_shapes=[pltpu.VMEM((heads, tq, LANES), F32), pltpu.VMEM((heads, tq, 1), F32),
                        pltpu.VMEM((tq, d), BF16)])
    return pl.pallas_call(
        functools.partial(_fox_attn_prompt_body, heads=heads, hd=hd),
        grid_spec=grid_spec, out_shape=jax.ShapeDtypeStruct(x_all.shape, F32),
        input_output_aliases={4: 0}, compiler_params=_params(1), name="fox_attn_prompt",
    )(jnp.asarray(qi), jnp.asarray(kj), jnp.asarray(first), jnp.asarray(last),
      x_all, q, kb, vb, f_nat, f_t, w_out)


def _fox_attn_sample_body(q_ref, kn_ref, vn_ref, ck_ref, cv_ref, lft_ref, o_ref,
                          qbd_ref, bias_ref, m_ref, l_ref, acc_ref, *, heads, n_chunks):
    s_new, d = q_ref.shape
    hd = d // heads
    rows = heads * s_new
    kc_len = ck_ref.shape[1]
    past = kc_len * n_chunks
    width = lft_ref.shape[2]
    kc = pl.program_id(1)
    row_h = lax.broadcasted_iota(jnp.int32, (rows, d), 0) // s_new
    head_mask = row_h == lax.broadcasted_iota(jnp.int32, (rows, d), 1) // hd

    @pl.when(kc == 0)
    def _():
        qt = jnp.concatenate([q_ref[...].astype(F32)] * heads, axis=0)
        qbd_ref[...] = jnp.where(head_mask, qt, 0.0).astype(BF16)
        lf = lft_ref[0]
        lane = lax.broadcasted_iota(jnp.int32, (heads, width), 1)
        run = lf
        shift = 1
        while shift < width:
            moved = pltpu.roll(run, width - shift, 1)
            run = run + jnp.where(lane + shift < width, moved, 0.0)
            shift *= 2
        rexcl = run - lf
        rexp = jnp.concatenate(
            [jnp.broadcast_to(rexcl[h:h + 1, :], (s_new, width)) for h in range(heads)], axis=0)
        tail = rexp[:, past:]
        tl = lax.broadcasted_iota(jnp.int32, tail.shape, 1)
        tr = lax.broadcasted_iota(jnp.int32, tail.shape, 0) % s_new
        rq = jnp.sum(jnp.where(tl == tr, tail, 0.0), axis=1, keepdims=True)
        bias_ref[...] = rexp - rq
        m_ref[...] = jnp.full_like(m_ref, NEG)
        l_ref[...] = jnp.zeros_like(l_ref)
        acc_ref[...] = jnp.zeros_like(acc_ref)

    def update(s, vmat):
        m_prev = m_ref[...]
        m_new = jnp.maximum(m_prev, jnp.max(s, axis=1, keepdims=True))
        alpha = jnp.exp(m_prev - m_new)
        p = jnp.exp(s - m_new)
        l_ref[...] = alpha * l_ref[...] + jnp.sum(p, axis=1, keepdims=True)
        acc_ref[...] = alpha * acc_ref[...] + _dot(p.astype(BF16), vmat)
        m_ref[...] = m_new

    for c in range(n_chunks):
        @pl.when(kc == c)
        def _(c=c):
            s = _dot_nt(qbd_ref[...], ck_ref[0].astype(BF16)) + bias_ref[:, c * kc_len:(c + 1) * kc_len]
            update(s, cv_ref[0].astype(BF16))

    @pl.when(kc == n_chunks - 1)
    def _():
        pad = jnp.zeros((width - past - s_new, d), BF16)
        kn = jnp.concatenate([kn_ref[...].astype(BF16), pad], axis=0)
        vn = jnp.concatenate([vn_ref[...].astype(BF16), pad], axis=0)
        s = _dot_nt(qbd_ref[...], kn) + bias_ref[:, past:]
        tl = lax.broadcasted_iota(jnp.int32, s.shape, 1)
        tr = lax.broadcasted_iota(jnp.int32, s.shape, 0) % s_new
        update(jnp.where(tl <= tr, s, NEG), vn)
        o_full = jnp.where(head_mask, acc_ref[...] / l_ref[...], 0.0)
        out = o_full[0:s_new, :]
        for h in range(1, heads):
            out = out + o_full[h * s_new:(h + 1) * s_new, :]
        o_ref[...] = out.astype(BF16)


def _fox_attn_sample(q_s, k_s, v_s, cache_k, cache_v, lft_all, *, heads, s_new):
    t_s, d = q_s.shape
    n_b = t_s // s_new
    past = cache_k.shape[1]
    n_chunks = 2 if past % 2048 == 0 else 1
    kc_len = past // n_chunks
    width = lft_all.shape[2]
    rows = heads * s_new
    return pl.pallas_call(
        functools.partial(_fox_attn_sample_body, heads=heads, n_chunks=n_chunks),
        grid=(n_b, n_chunks),
        in_specs=[
            pl.BlockSpec((s_new, d), lambda b, c: (b, 0)),
            pl.BlockSpec((s_new, d), lambda b, c: (b, 0)),
            pl.BlockSpec((s_new, d), lambda b, c: (b, 0)),
            pl.BlockSpec((1, kc_len, d), lambda b, c: (b, c, 0)),
            pl.BlockSpec((1, kc_len, d), lambda b, c: (b, c, 0)),
            pl.BlockSpec((1, heads, width), lambda b, c: (b, 0, 0)),
        ],
        out_specs=pl.BlockSpec((s_new, d), lambda b, c: (b, 0)),
        out_shape=jax.ShapeDtypeStruct((t_s, d), BF16),
        scratch_shapes=[pltpu.VMEM((rows, d), BF16), pltpu.VMEM((rows, width), F32),
                        pltpu.VMEM((rows, 1), F32), pltpu.VMEM((rows, 1), F32),
                        pltpu.VMEM((rows, d), F32)],
        compiler_params=_params(2), name="fox_attn_sample")(q_s, k_s, v_s, cache_k, cache_v, lft_all)


def _proj_residual_body(x_ref, o_ref_in, w_ref, o_ref):
    o_ref[...] = x_ref[...] + _dot(o_ref_in[...], w_ref[...])


def _sgu_body(x_ref, g_ref, win_ref, lg_ref, lb_ref, ws_ref, bs_ref, wo_ref, o_ref, v_ref, mix_ref,
              *, n_prompt_tiles, groups):
    tm, d = x_ref.shape
    gc = d // groups
    i = pl.program_id(0)
    sel = (i >= n_prompt_tiles).astype(jnp.int32)
    hn = _rms(x_ref[...], g_ref[...]).astype(BF16)
    z = _dot(hn, win_ref[...])
    z = 0.5 * z * (1.0 + lax.erf(z * (2.0 ** -0.5)))
    u = z[:, :d]
    v = _ln(z[:, d:], lg_ref[...], lb_ref[...])
    v_ref[...] = v
    vb = v.astype(BF16)
    for c in range(tm // SGU_CHUNK):
        rs = slice(c * SGU_CHUNK, (c + 1) * SGU_CHUNK)
        for g in range(groups):
            cs = slice(g * gc, (g + 1) * gc)
            mix_ref[rs, cs] = _dot(ws_ref[sel, g], vb[rs, cs])
        mix_ref[rs, :] = mix_ref[rs, :] + bs_ref[sel]
    o_ref[...] = x_ref[...] + _dot((u * mix_ref[...]).astype(BF16), wo_ref[...])


def _xattn_core(q, k, v, heads):
    d = q.shape[1]
    hd = d // heads
    outs = []
    for h in range(heads):
        sl = slice(h * hd, (h + 1) * hd)
        s = _dot_nt(q[:, sl], k[:, sl])
        p = jnp.exp(s - jnp.max(s, axis=1, keepdims=True))
        p = p / jnp.sum(p, axis=1, keepdims=True)
        outs.append(_dot(p.astype(BF16), v[:, sl]).astype(BF16))
    return outs


def _xattn_prompt_body(x_ref, g_ref, wq_ref, k_ref, v_ref, wo_ref, o_ref, *, heads, layer):
    d = x_ref.shape[1]
    hd = d // heads
    x = x_ref[...]
    q = _dot(_rms(x, g_ref[...]).astype(BF16), wq_ref[...]).astype(BF16)
    k = k_ref[layer].astype(BF16)
    v = v_ref[layer].astype(BF16)
    y = x
    for h, oh in enumerate(_xattn_core(q, k, v, heads)):
        y = y + _dot(oh, wo_ref[h * hd:(h + 1) * hd, :])
    o_ref[...] = y


def _xattn_sample_body(x_ref, g_ref, wq_ref, k_ref, v_ref, wo_ref, o_ref, ob_ref, *, heads, s_new):
    d = x_ref.shape[1]
    hd = d // heads
    bb = k_ref.shape[1]
    x = x_ref[...]
    q = _dot(_rms(x, g_ref[...]).astype(BF16), wq_ref[...]).astype(BF16)
    for b in range(bb):
        rs = slice(b * s_new, (b + 1) * s_new)
        outs = _xattn_core(q[rs, :], k_ref[0, b].astype(BF16), v_ref[0, b].astype(BF16), heads)
        for h, oh in enumerate(outs):
            ob_ref[rs, h * hd:(h + 1) * hd] = oh
    o_ref[...] = x + _dot(ob_ref[...], wo_ref[...])


def _ffn_body(x_ref, g_ref, wg_ref, wu_ref, wd_ref, o_ref, hn_ref, acc_ref):
    f = pl.program_id(1)

    @pl.when(f == 0)
    def _():
        hn_ref[...] = _rms(x_ref[...], g_ref[...]).astype(BF16)
        acc_ref[...] = jnp.zeros_like(acc_ref)

    hn = hn_ref[...]
    a = _silu(_dot(hn, wg_ref[...])) * _dot(hn, wu_ref[...])
    acc_ref[...] += _dot(a.astype(BF16), wd_ref[...])

    @pl.when(f == pl.num_programs(1) - 1)
    def _():
        o_ref[...] = x_ref[...] + acc_ref[...]


def _ffn(x_all, g, w_gu, w_down, *, tm, tf):
    m, d = x_all.shape
    ff = w_down.shape[0]
    nf = ff // tf
    return pl.pallas_call(
        _ffn_body, grid=(m // tm, nf),
        in_specs=[pl.BlockSpec((tm, d), lambda i, f: (i, 0)),
                  _full_spec(g),
                  pl.BlockSpec((d, tf), lambda i, f: (0, f)),
                  pl.BlockSpec((d, tf), lambda i, f, nf=nf: (0, nf + f)),
                  pl.BlockSpec((tf, d), lambda i, f: (f, 0))],
        out_specs=pl.BlockSpec((tm, d), lambda i, f: (i, 0)),
        out_shape=jax.ShapeDtypeStruct((m, d), F32),
        input_output_aliases={0: 0},
        scratch_shapes=[pltpu.VMEM((tm, d), BF16), pltpu.VMEM((tm, d), F32)],
        compiler_params=_params(2), name="ffn")(x_all, g, w_gu, w_gu, w_down)


def _router_body(x_ref, g_ref, rh_ref, rl_ref, hn_ref, idx_ref, gate_ref, *, n_experts):
    hn32 = _rms(x_ref[...], g_ref[...])
    hn_ref[...] = hn32
    logits = _dot_hi(hn32, rh_ref[...], rl_ref[...])
    lane = lax.broadcasted_iota(jnp.int32, logits.shape, 1)
    lane_f = lane.astype(F32)
    big = float(LANES)
    masked = jnp.where(lane < n_experts, logits, -jnp.inf)
    m1 = jnp.max(masked, axis=1, keepdims=True)
    i1 = jnp.min(jnp.where(masked == m1, lane_f, big), axis=1, keepdims=True)
    masked2 = jnp.where(lane_f == i1, -jnp.inf, masked)
    m2 = jnp.max(masked2, axis=1, keepdims=True)
    i2 = jnp.min(jnp.where(masked2 == m2, lane_f, big), axis=1, keepdims=True)
    e = jnp.exp(m2 - m1)
    w1 = 1.0 / (1.0 + e)
    w2 = e * w1
    idx_ref[...] = jnp.where(lane == 0, i1, jnp.where(lane == 1, i2, 0.0)).astype(jnp.int32)
    gate_ref[...] = jnp.where(lane == 0, w1, jnp.where(lane == 1, w2, 0.0))


def _gather_rows(src_hbm, idx_ref, base, dst_ref, sem, n, stride=1, offset=0):
    def copy(j):
        row = idx_ref[base + j * stride + offset]
        return pltpu.make_async_copy(src_hbm.at[pl.ds(row, 1)], dst_ref.at[pl.ds(j, 1)], sem)

    def issue(j, c):
        copy(j).start()
        return c

    def wait(j, c):
        copy(j).wait()
        return c

    lax.fori_loop(0, n, issue, 0, unroll=8)
    lax.fori_loop(0, n, wait, 0, unroll=8)


def _moe_ffn_body(te_ref, nt_ref, tok_ref, hn_hbm, wg_ref, wu_ref, wd_ref, o_ref,
                  hbuf_ref, hb_ref, acc_ref, sem):
    tm = hbuf_ref.shape[0]
    r = pl.program_id(0)
    f = pl.program_id(1)
    valid = r < nt_ref[0]

    @pl.when(jnp.logical_and(valid, f == 0))
    def _():
        _gather_rows(hn_hbm, tok_ref, r * tm, hbuf_ref, sem, tm)
        hb_ref[...] = hbuf_ref[...].astype(BF16)
        acc_ref[...] = jnp.zeros_like(acc_ref)

    @pl.when(valid)
    def _():
        h = hb_ref[...]
        a = _silu(_dot(h, wg_ref[0])) * _dot(h, wu_ref[0])
        acc_ref[...] += _dot(a.astype(BF16), wd_ref[0])

    last = f == pl.num_programs(1) - 1

    @pl.when(jnp.logical_and(valid, last))
    def _():
        o_ref[...] = acc_ref[...]

    @pl.when(jnp.logical_and(jnp.logical_not(valid), last))
    def _():
        o_ref[...] = jnp.zeros_like(o_ref)


def _moe_ffn(hn, tile_expert, n_tiles, row_token, w_gu, w_down, *, tm, tf, r_max):
    d = hn.shape[1]
    d_exp = w_down.shape[1]
    nf = d_exp // tf

    def fe(r, f, nt):
        return jnp.where(r < nt[0], f, nf - 1)

    grid_spec = pltpu.PrefetchScalarGridSpec(
        num_scalar_prefetch=3, grid=(r_max, nf),
        in_specs=[
            pl.BlockSpec(memory_space=pl.ANY),
            pl.BlockSpec((1, d, tf), lambda r, f, te, nt, tok: (te[r], 0, fe(r, f, nt))),
            pl.BlockSpec((1, d, tf), lambda r, f, te, nt, tok: (te[r], 0, nf + fe(r, f, nt))),
            pl.BlockSpec((1, tf, d), lambda r, f, te, nt, tok: (te[r], fe(r, f, nt), 0)),
        ],
        out_specs=pl.BlockSpec((tm, d), lambda r, f, te, nt, tok: (r, 0)),
        scratch_shapes=[pltpu.VMEM((tm, d), F32), pltpu.VMEM((tm, d), BF16), pltpu.VMEM((tm, d), F32),
                        pltpu.SemaphoreType.DMA(())])
    return pl.pallas_call(
        _moe_ffn_body, grid_spec=grid_spec,
        out_shape=jax.ShapeDtypeStruct((r_max * tm, d), F32),
        compiler_params=_params(2), name="moe_ffn")(tile_expert, n_tiles, row_token, hn, w_gu, w_gu, w_down)


def _moe_combine_body(dest_ref, x_ref, gate_ref, y_hbm, o_ref, b0_ref, b1_ref, sem0, sem1):
    tm = x_ref.shape[0]
    base = pl.program_id(0) * (tm * TOP_K)
    _gather_rows(y_hbm, dest_ref, base, b0_ref, sem0, tm, stride=TOP_K, offset=0)
    _gather_rows(y_hbm, dest_ref, base, b1_ref, sem1, tm, stride=TOP_K, offset=1)
    gate = gate_ref[...]
    o_ref[...] = x_ref[...] + gate[:, 0:1] * b0_ref[...] + gate[:, 1:2] * b1_ref[...]


def _moe_combine(x_all, gates, dest, y_sorted, *, tm):
    m, d = x_all.shape
    grid_spec = pltpu.PrefetchScalarGridSpec(
        num_scalar_prefetch=1, grid=(m // tm,),
        in_specs=[pl.BlockSpec((tm, d), lambda i, dest: (i, 0)),
                  pl.BlockSpec((tm, LANES), lambda i, dest: (i, 0)),
                  pl.BlockSpec(memory_space=pl.ANY)],
        out_specs=pl.BlockSpec((tm, d), lambda i, dest: (i, 0)),
        scratch_shapes=[pltpu.VMEM((tm, d), F32), pltpu.VMEM((tm, d), F32),
                        pltpu.SemaphoreType.DMA(()), pltpu.SemaphoreType.DMA(())])
    return pl.pallas_call(
        _moe_combine_body, grid_spec=grid_spec, out_shape=jax.ShapeDtypeStruct((m, d), F32),
        input_output_aliases={1: 0}, compiler_params=_params(1), name="moe_combine",
    )(dest, x_all, gates, y_sorted)


def _moe_plan(top_i, n_experts, tm, r_max):
    flat = top_i.reshape(-1)
    onehot = (flat[:, None] == jnp.arange(n_experts, dtype=jnp.int32)[None, :]).astype(jnp.int32)
    rank = jnp.take_along_axis(jnp.cumsum(onehot, axis=0), flat[:, None], axis=1)[:, 0] - 1
    counts = jnp.sum(onehot, axis=0)
    tiles = (counts + tm - 1) // tm
    tile_end = jnp.cumsum(tiles)
    row_start = (tile_end - tiles) * tm
    dest = (row_start[flat] + rank).astype(jnp.int32)
    n_tiles = tile_end[-1:].astype(jnp.int32)
    tile_expert = jnp.minimum(
        jnp.searchsorted(tile_end, jnp.arange(r_max, dtype=jnp.int32), side="right"),
        n_experts - 1).astype(jnp.int32)
    row_token = jnp.zeros((r_max * tm,), jnp.int32).at[dest].set(
        jnp.arange(flat.shape[0], dtype=jnp.int32) // TOP_K)
    return dest, n_tiles, tile_expert, row_token


def _final_norm_body(x_ref, g_ref, o_ref):
    o_ref[...] = _rms(x_ref[...], g_ref[...])


def _pick(n, cands):
    for c in cands:
        if n % c == 0:
            return c
    raise ValueError(f"no tile in {cands} divides {n}")


def kernel(x_prompt, x_sample, mem_prompt, cache_mem_k, cache_mem_v, state_conv, cache_fox_k, cache_fox_v, cache_fox_logf, norm_mix, norm_mem_q, norm_ffn, norm_mem, norm_final, conv_w_in, conv_dw, conv_dw_b, conv_ln_g, conv_ln_b, conv_w_out, fox_w_in, fox_b_f, fox_w_out, sgu_w_in, sgu_ln_g, sgu_ln_b, sgu_w_s, sgu_b_s, sgu_w_out, xa_w_q, xa_w_kv, xa_w_o, ffn_w_gu, ffn_w_down, moe_router, moe_w_gu, moe_w_down):
    batch, t_p, d = x_prompt.shape
    n_b, s_new, _ = x_sample.shape
    assert batch == 1
    t_s = n_b * s_new
    m = t_p + t_s
    depth = norm_mix.shape[0]
    n_mem = mem_prompt.shape[1]
    xa_heads = cache_mem_k.shape[3]
    taps = conv_dw.shape[1]
    fox_heads = fox_b_f.shape[1]
    past = cache_fox_k.shape[2]
    groups = sgu_w_s.shape[1]
    n_experts = moe_router.shape[2]
    d_exp = moe_w_down.shape[2]
    d_ff = ffn_w_down.shape[1]

    tm = _pick(int(np.gcd(t_p, t_s)), (512, 256, 128))
    n_pt = t_p // tm
    n_st = t_s // tm
    n_t = n_pt + n_st
    halo = 32
    assert taps - 1 <= halo and tm % halo == 0 and tm % SGU_CHUNK == 0 and SGU_CHUNK % s_new == 0
    assert fox_heads % 2 == 0 and d // fox_heads * 2 == LANES and fox_heads <= LANES
    assert past % SGU_CHUNK == 0 and s_new <= SGU_BLOCK

    row = lambda a: a.reshape(1, -1)
    x_all = jnp.concatenate([x_prompt[0], x_sample.reshape(t_s, d)], axis=0)

    mem_k, mem_v = _mem_kv(mem_prompt[0], row(norm_mem), xa_w_kv.astype(BF16))
    cache_k4 = cache_mem_k.reshape(depth, n_b, n_mem, d)
    cache_v4 = cache_mem_v.reshape(depth, n_b, n_mem, d)

    conv_p, conv_s = [], []
    fox_out = []
    sgu_v = []
    for i in range(depth):
        kind = i % 3
        j = i // 3
        g_mix = row(norm_mix[i])
        if kind == 0:
            (u_all,) = _row_call(
                _conv_in_body, grid_tiles=n_t, tm=tm, first_tile=0, row_ins=[x_all],
                full_ins=[g_mix, conv_w_in[j].astype(BF16)], outs=[(m, d, F32, 0)], alias_x=False,
                name="conv_in")
            dw = jnp.concatenate([conv_dw[j], jnp.zeros((halo - taps, d), F32)], axis=0)
            tail_w = [dw, row(conv_dw_b[j]), row(conv_ln_g[j]), row(conv_ln_b[j]), conv_w_out[j].astype(BF16)]
            hb = tm // halo
            x_all = pl.pallas_call(
                functools.partial(_conv_prompt_body, taps=taps, halo=halo, rc=16), grid=(n_pt,),
                in_specs=[pl.BlockSpec((tm, d), lambda t: (t, 0)),
                          pl.BlockSpec((tm, d), lambda t: (t, 0)),
                          pl.BlockSpec((halo, d), lambda t, hb=hb: (jnp.maximum(t * hb - 1, 0), 0))]
                         + [_full_spec(a) for a in tail_w],
                out_specs=pl.BlockSpec((tm, d), lambda t: (t, 0)),
                out_shape=jax.ShapeDtypeStruct((m, d), F32), input_output_aliases={0: 0},
                scratch_shapes=[pltpu.VMEM((tm + halo + 8, d), F32),
                                pltpu.VMEM((8, tm + 8 * ((taps - 1) // 8), d), F32),
                                pltpu.VMEM((tm, d), F32)],
                compiler_params=_params(1), name="conv_prompt")(x_all, u_all, u_all, *tail_w)
            u_s = u_all[t_p:].reshape(n_b, s_new, d)
            ext_s = jnp.concatenate([state_conv[j], u_s], axis=1)
            bb = _pick(n_b, (8, 4, 2, 1))
            rows_s = bb * s_new
            x_all = pl.pallas_call(
                functools.partial(_conv_sample_body, taps=taps), grid=(n_b // bb,),
                in_specs=[pl.BlockSpec((rows_s, d), lambda t, o=t_p // rows_s: (t + o, 0)),
                          pl.BlockSpec((bb, taps - 1 + s_new, d), lambda t: (t, 0, 0))]
                         + [_full_spec(a) for a in tail_w],
                out_specs=pl.BlockSpec((rows_s, d), lambda t, o=t_p // rows_s: (t + o, 0)),
                out_shape=jax.ShapeDtypeStruct((m, d), F32), input_output_aliases={0: 0},
                scratch_shapes=[pltpu.VMEM((rows_s, d), F32)],
                compiler_params=_params(1), name="conv_sample")(x_all, ext_s, *tail_w)
            conv_p.append(u_all[t_p - (taps - 1):t_p][None])
            conv_s.append(ext_s[:, s_new:])
        elif kind == 1:
            hd = d // fox_heads
            w_in = fox_w_in[j]
            w_qkv = jnp.concatenate([w_in[:, :d] * (hd ** -0.5), w_in[:, d:3 * d]], axis=1).astype(BF16)
            w_f = jnp.concatenate([w_in[:, 3 * d:], jnp.zeros((d, LANES - fox_heads), F32)], axis=1)
            wf_hi = w_f.astype(BF16)
            wf_lo = (w_f - wf_hi.astype(F32)).astype(BF16)
            b_f = jnp.concatenate([fox_b_f[j], jnp.zeros((LANES - fox_heads,), F32)])
            b_row = row(b_f)
            b_col = jnp.broadcast_to(b_f[:, None], (LANES, tm))
            proj_w = [g_mix, w_qkv, wf_hi, wf_lo, b_row]
            q_p, k_p, v_p, kb_p, vb_p, lf_p, f_nat, f_t = _fox_proj_prompt(
                x_all, proj_w + [wf_hi.T, wf_lo.T, b_col], t_p=t_p, tm=tm, heads=fox_heads)
            w_out = fox_w_out[j].astype(BF16)
            x_all = _fox_attn_prompt(x_all, q_p, kb_p, vb_p, f_nat, f_t, w_out, t_p=t_p, tq=tm,
                                     heads=fox_heads)
            q_s, k_s, v_s, lf_s = _row_call(
                _fox_proj_sample_body, grid_tiles=n_st, tm=tm, first_tile=n_pt, row_ins=[x_all],
                full_ins=proj_w,
                outs=[(t_s, d, BF16, 0), (t_s, d, F32, 0), (t_s, d, F32, 0), (t_s, fox_heads, F32, 0)],
                alias_x=False, name="fox_proj_sample")
            width = past + LANES
            lft_all = jnp.concatenate(
                [jnp.swapaxes(cache_fox_logf[j], 1, 2),
                 jnp.swapaxes(lf_s.reshape(n_b, s_new, fox_heads), 1, 2),
                 jnp.zeros((n_b, fox_heads, width - past - s_new), F32)], axis=2)
            o_s = _fox_attn_sample(q_s, k_s, v_s, cache_fox_k[j].reshape(n_b, past, d),
                                   cache_fox_v[j].reshape(n_b, past, d), lft_all,
                                   heads=fox_heads, s_new=s_new)
            x_all = pl.pallas_call(
                _proj_residual_body, grid=(n_st,),
                in_specs=[pl.BlockSpec((tm, d), lambda t, o=n_pt: (t + o, 0)),
                          pl.BlockSpec((tm, d), lambda t: (t, 0)), _full_spec(w_out)],
                out_specs=pl.BlockSpec((tm, d), lambda t, o=n_pt: (t + o, 0)),
                out_shape=jax.ShapeDtypeStruct((m, d), F32), input_output_aliases={0: 0},
                compiler_params=_params(1), name="fox_out_sample")(x_all, o_s, w_out)
            fox_out.append((k_p, v_p, lf_p, k_s, v_s, lf_s))
        else:
            ws = sgu_w_s[j]
            pos = jnp.arange(SGU_CHUNK)
            blk_mask = (pos[None, :] // SGU_BLOCK) <= (pos[:, None] // SGU_BLOCK)
            ws_p = jnp.where(blk_mask[None], ws, 0.0)
            reps = SGU_CHUNK // s_new
            eye = jnp.eye(reps, dtype=F32)
            ws_s = jnp.einsum("ab,gij->gaibj", eye, ws_p[:, :s_new, :s_new]).reshape(
                groups, SGU_CHUNK, SGU_CHUNK)
            ws_all = jnp.stack([ws_p, ws_s]).astype(BF16)
            gc = d // groups
            bs_p = jnp.repeat(sgu_b_s[j].T, gc, axis=1)
            bs_s = jnp.tile(bs_p[:s_new], (reps, 1))
            bs_all = jnp.stack([bs_p, bs_s])
            x_all, v_all = _row_call(
                functools.partial(_sgu_body, n_prompt_tiles=n_pt, groups=groups),
                grid_tiles=n_t, tm=tm, first_tile=0, row_ins=[x_all],
                full_ins=[g_mix, sgu_w_in[j].astype(BF16), row(sgu_ln_g[j]), row(sgu_ln_b[j]),
                          ws_all, bs_all, sgu_w_out[j].astype(BF16)],
                outs=[(m, d, F32, 0), (m, d, F32, 0)], scratch=[pltpu.VMEM((tm, d), F32)], name="sgu")
            sgu_v.append(v_all[t_p:].reshape(n_b, s_new, d))

        xa_hd = d // xa_heads
        wq = (xa_w_q[i] * (xa_hd ** -0.5)).astype(BF16)
        wo = xa_w_o[i].astype(BF16)
        g_q = row(norm_mem_q[i])
        (x_all,) = _row_call(
            functools.partial(_xattn_prompt_body, heads=xa_heads, layer=i),
            grid_tiles=n_pt, tm=tm, first_tile=0, row_ins=[x_all],
            full_ins=[g_q, wq, mem_k, mem_v, wo], outs=[(m, d, F32, 0)], name="xattn_prompt")
        bb = _pick(n_b, (4, 2, 1))
        rows_s = bb * s_new
        x_all = pl.pallas_call(
            functools.partial(_xattn_sample_body, heads=xa_heads, s_new=s_new), grid=(n_b // bb,),
            in_specs=[pl.BlockSpec((rows_s, d), lambda t, o=t_p // rows_s: (t + o, 0)),
                      _full_spec(g_q), _full_spec(wq),
                      pl.BlockSpec((1, bb, n_mem, d), lambda t, i=i: (i, t, 0, 0)),
                      pl.BlockSpec((1, bb, n_mem, d), lambda t, i=i: (i, t, 0, 0)),
                      _full_spec(wo)],
            out_specs=pl.BlockSpec((rows_s, d), lambda t, o=t_p // rows_s: (t + o, 0)),
            out_shape=jax.ShapeDtypeStruct((m, d), F32), input_output_aliases={0: 0},
            scratch_shapes=[pltpu.VMEM((rows_s, d), BF16)],
            compiler_params=_params(1), name="xattn_sample")(x_all, g_q, wq, cache_k4, cache_v4, wo)

        g_f = row(norm_ffn[i])
        c = i // 2
        if i % 2 == 0:
            tf = _pick(d_ff, (1408, 1024, 512, 256, 128))
            x_all = _ffn(x_all, g_f, ffn_w_gu[c].astype(BF16), ffn_w_down[c].astype(BF16), tm=tm, tf=tf)
        else:
            router = jnp.concatenate([moe_router[c], jnp.zeros((d, LANES - n_experts), F32)], axis=1)
            r_hi = router.astype(BF16)
            r_lo = (router - r_hi.astype(F32)).astype(BF16)
            hn, idx, gates = _row_call(
                functools.partial(_router_body, n_experts=n_experts), grid_tiles=n_t, tm=tm,
                first_tile=0, row_ins=[x_all], full_ins=[g_f, r_hi, r_lo],
                outs=[(m, d, F32, 0), (m, LANES, jnp.int32, 0), (m, LANES, F32, 0)], alias_x=False,
                name="moe_router")
            r_max = (TOP_K * m + n_experts * (tm - 1)) // tm
            dest, n_tiles, tile_expert, row_token = _moe_plan(idx[:, :TOP_K], n_experts, tm, r_max)
            tf = _pick(d_exp, (1792, 1024, 512, 256, 128))
            y_sorted = _moe_ffn(hn, tile_expert, n_tiles, row_token, moe_w_gu[c].astype(BF16),
                                moe_w_down[c].astype(BF16), tm=tm, tf=tf, r_max=r_max)
            x_all = _moe_combine(x_all, gates, dest, y_sorted, tm=tm)

    (y_all,) = _row_call(_final_norm_body, grid_tiles=n_t, tm=tm, first_tile=0, row_ins=[x_all],
                         full_ins=[row(norm_final)], outs=[(m, d, F32, 0)], alias_x=False,
                         name="final_norm")
    y_prompt = y_all[:t_p][None]
    y_sample = y_all[t_p:].reshape(n_b, s_new, d)
    mem_shape = (depth, 1, n_mem, xa_heads, d // xa_heads)
    fk_p, fv_p, fl_p, fk_s, fv_s, fl_s = zip(*fox_out)
    hshape_p = (1, t_p, fox_heads, d // fox_heads)
    hshape_s = (n_b, s_new, fox_heads, d // fox_heads)
    return (y_prompt, y_sample, mem_k.reshape(mem_shape), mem_v.reshape(mem_shape),
            jnp.stack(conv_p), jnp.stack(conv_s),
            jnp.stack([a.reshape(hshape_p) for a in fk_p]), jnp.stack([a.reshape(hshape_p) for a in fv_p]),
            jnp.stack([a.reshape(1, t_p, fox_heads) for a in fl_p]),
            jnp.stack([a.reshape(hshape_s) for a in fk_s]), jnp.stack([a.reshape(hshape_s) for a in fv_s]),
            jnp.stack([a.reshape(n_b, s_new, fox_heads) for a in fl_s]),
            jnp.stack(sgu_v))
```

```python
import functools

import numpy as np
import jax
import jax.numpy as jnp
from jax import lax
from jax.experimental import pallas as pl
from jax.experimental.pallas import tpu as pltpu

F32 = jnp.float32
BF16 = jnp.bfloat16
RMS_EPS = 1e-6
LN_EPS = 1e-5
SGU_CHUNK = 128
SGU_BLOCK = 64
TOP_K = 2
LANES = 128
NEG = -1e30
VMEM_LIMIT = 52 * 1024 * 1024
FOX_TILE = 256


def _params(n_axes, vmem=VMEM_LIMIT):
    return pltpu.CompilerParams(dimension_semantics=("arbitrary",) * n_axes, vmem_limit_bytes=vmem)


def _dot(a, b):
    return jnp.dot(a, b, preferred_element_type=F32)


def _dot_nt(a, b):
    return lax.dot_general(a, b, (((1,), (1,)), ((), ())), preferred_element_type=F32)


def _rms(x, g):
    return x * lax.rsqrt(jnp.mean(x * x, axis=-1, keepdims=True) + RMS_EPS) * g


def _ln(x, g, b):
    mu = jnp.mean(x, axis=-1, keepdims=True)
    xc = x - mu
    var = jnp.mean(xc * xc, axis=-1, keepdims=True)
    return xc * lax.rsqrt(var + LN_EPS) * g + b


def _silu(x):
    return x * jax.nn.sigmoid(x)


def _log_sigmoid(x):
    return jnp.minimum(x, 0.0) - jnp.log(1.0 + jnp.exp(-jnp.abs(x)))


def _split3(x):
    a = x.astype(BF16)
    r = x - a.astype(F32)
    b = r.astype(BF16)
    c = (r - b.astype(F32)).astype(BF16)
    return a, b, c


def _split2(x):
    a = x.astype(BF16)
    return a, (x - a.astype(F32)).astype(BF16)


def _dot_hi(x32, w_hi, w_lo):
    xh, xl = _split2(x32)
    return _dot(xh, w_hi) + _dot(xl, w_hi) + _dot(xh, w_lo)


def _full_spec(a):
    nd = a.ndim
    return pl.BlockSpec(a.shape, lambda *_, nd=nd: (0,) * nd)


def _row_call(body, *, grid_tiles, tm, first_tile, row_ins, full_ins, outs, alias_x=True,
              scratch=(), name=None):
    in_specs = [pl.BlockSpec((tm, a.shape[1]), lambda i, ft=first_tile: (i + ft, 0)) for a in row_ins]
    in_specs += [_full_spec(a) for a in full_ins]
    out_shape, out_specs = [], []
    for rows, cols, dt, off in outs:
        out_shape.append(jax.ShapeDtypeStruct((rows, cols), dt))
        out_specs.append(pl.BlockSpec((tm, cols), lambda i, off=off: (i + off, 0)))
    return pl.pallas_call(
        body, grid=(grid_tiles,), in_specs=in_specs, out_specs=out_specs, out_shape=out_shape,
        input_output_aliases={0: 0} if alias_x else {}, scratch_shapes=list(scratch),
        compiler_params=_params(1), name=name)(*row_ins, *full_ins)


def _mem_kv_body(mem_ref, g_ref, w_ref, k_ref, v_ref):
    d = mem_ref.shape[1]
    mn = _rms(mem_ref[...], g_ref[...]).astype(BF16)
    kv = _dot(mn, w_ref[0])
    k_ref[0] = kv[:, :d]
    v_ref[0] = kv[:, d:]


def _mem_kv(mem, g, w_kv):
    depth, d, _ = w_kv.shape
    n_mem = mem.shape[0]
    return pl.pallas_call(
        _mem_kv_body, grid=(depth,),
        in_specs=[_full_spec(mem), _full_spec(g), pl.BlockSpec((1, d, 2 * d), lambda i: (i, 0, 0))],
        out_specs=[pl.BlockSpec((1, n_mem, d), lambda i: (i, 0, 0))] * 2,
        out_shape=[jax.ShapeDtypeStruct((depth, n_mem, d), F32)] * 2,
        compiler_params=_params(1), name="mem_kv")(mem, g, w_kv)


def _conv_in_body(x_ref, g_ref, w_ref, u_ref):
    d = u_ref.shape[1]
    hn = _rms(x_ref[...], g_ref[...]).astype(BF16)
    z = _dot(hn, w_ref[...])
    u_ref[...] = z[:, :d] * jax.nn.sigmoid(z[:, d:])


def _conv_tail(c, lg, lb, wo):
    c = _silu(_ln(c, lg, lb))
    return _dot(c.astype(BF16), wo)


def _conv_prompt_body(x_ref, u_ref, halo_ref, dw8_ref, dwb_ref, lg_ref, lb_ref, wo_ref, o_ref,
                      ext_ref, sh_ref, c_ref, *, taps, halo, rc):
    tm, d = u_ref.shape
    i = pl.program_id(0)
    ext_ref[0:halo, :] = jnp.where(i == 0, 0.0, halo_ref[...])
    ext_ref[halo:halo + tm, :] = u_ref[...]
    ext_ref[halo + tm:, :] = jnp.zeros((ext_ref.shape[0] - halo - tm, d), F32)
    off = halo - (taps - 1)
    n_sh = sh_ref.shape[1]
    for b in range(8):
        sh_ref[b] = ext_ref[pl.ds(off + b, n_sh), :]

    def chunk(r, carry):
        r0 = pl.multiple_of(r * rc, rc)
        acc = jnp.zeros((rc, d), F32) + dwb_ref[...]
        for k in range(taps):
            a, b = divmod(k, 8)
            w = jnp.concatenate([dw8_ref[k]] * (rc // 8), axis=0)
            acc = acc + sh_ref[b, pl.ds(r0 + 8 * a, rc), :] * w
        c_ref[pl.ds(r0, rc), :] = acc
        return carry

    lax.fori_loop(0, tm // rc, chunk, 0)
    o_ref[...] = x_ref[...] + _conv_tail(c_ref[...], lg_ref[...], lb_ref[...], wo_ref[...])


def _conv_sample_body(x_ref, ext_ref, dw_ref, dwb_ref, lg_ref, lb_ref, wo_ref, o_ref, c_ref, *, taps):
    bb, ext_len, d = ext_ref.shape
    s = ext_len - (taps - 1)
    for b in range(bb):
        acc = jnp.zeros((s, d), F32) + dwb_ref[...]
        for k in range(taps):
            acc = acc + ext_ref[b, k:k + s, :] * dw_ref[k:k + 1, :]
        c_ref[b * s:(b + 1) * s, :] = acc
    o_ref[...] = x_ref[...] + _conv_tail(c_ref[...], lg_ref[...], lb_ref[...], wo_ref[...])


def _fox_qkv(x_ref, g_ref, wqkv_ref, wfh_ref, wfl_ref, bf_ref):
    d = x_ref.shape[1]
    hn32 = _rms(x_ref[...], g_ref[...])
    qkv = _dot(hn32.astype(BF16), wqkv_ref[...])
    lg = _dot_hi(hn32, wfh_ref[...], wfl_ref[...]) + bf_ref[...]
    return hn32, qkv[:, :d], qkv[:, d:2 * d], qkv[:, 2 * d:], _log_sigmoid(lg)


def _fox_proj_sample_body(x_ref, g_ref, wqkv_ref, wfh_ref, wfl_ref, bf_ref,
                          q_ref, k_ref, v_ref, lf_ref):
    _, q, k, v, logf = _fox_qkv(x_ref, g_ref, wqkv_ref, wfh_ref, wfl_ref, bf_ref)
    q_ref[...] = q.astype(BF16)
    k_ref[...] = k
    v_ref[...] = v
    lf_ref[...] = logf[:, :lf_ref.shape[1]]


def _fox_proj_prompt_body(x_ref, g_ref, wqkv_ref, wfh_ref, wfl_ref, bf_ref, wfth_ref, wftl_ref, bfc_ref,
                          hsel_ref, q_ref, k_ref, v_ref, kb_ref, vb_ref, lf_ref, f_ref, ft_ref, st_ref,
                          crow_ref, ccol_ref):
    tm = x_ref.shape[0]
    heads = lf_ref.shape[1]
    i = pl.program_id(0)

    @pl.when(i == 0)
    def _():
        crow_ref[...] = jnp.zeros_like(crow_ref)
        ccol_ref[...] = jnp.zeros_like(ccol_ref)

    hn32, q, k, v, logf = _fox_qkv(x_ref, g_ref, wqkv_ref, wfh_ref, wfl_ref, bf_ref)
    q_ref[...] = q.astype(BF16)
    k_ref[...] = k
    v_ref[...] = v
    kb_ref[...] = k.astype(BF16)
    vb_ref[...] = v.astype(BF16)
    lf_ref[...] = logf[:, :heads]

    lane = lax.broadcasted_iota(jnp.int32, (tm, LANES), 1)
    logf = jnp.where(lane < heads, logf, 0.0)
    hh, hl = _split2(hn32)
    lgt = _dot_nt(wfth_ref[...], hh) + _dot_nt(wfth_ref[...], hl) + _dot_nt(wftl_ref[...], hh) + bfc_ref[...]
    sub = lax.broadcasted_iota(jnp.int32, (LANES, tm), 0)
    logft = jnp.where(sub < heads, _log_sigmoid(lgt), 0.0)

    r = lax.broadcasted_iota(jnp.int32, (tm, tm), 0)
    c = lax.broadcasted_iota(jnp.int32, (tm, tm), 1)
    lower = jnp.where(c <= r, 1.0, 0.0).astype(BF16)
    upper = jnp.where(r <= c, 1.0, 0.0).astype(BF16)
    ones_r = jnp.ones((8, tm), BF16)
    ones_c = jnp.ones((tm, LANES), BF16)
    a1, a2, a3 = _split3(logf)
    b1, b2, b3 = _split3(logft)
    f_nat = _dot(lower, a1) + _dot(lower, a2) + _dot(lower, a3) + crow_ref[0:1, :]
    f_ref[...] = f_nat
    ft = _dot(b1, upper) + _dot(b2, upper) + _dot(b3, upper)
    ft_ref[...] = ft + jnp.tile(ccol_ref[...], (1, tm // LANES))
    crow_ref[...] = crow_ref[...] + _dot(ones_r, a1) + _dot(ones_r, a2) + _dot(ones_r, a3)
    ccol_ref[...] = ccol_ref[...] + _dot(b1, ones_c) + _dot(b2, ones_c) + _dot(b3, ones_c)

    def max_head_norm2(z):
        zh, zl = _split2(z * z)
        n2 = _dot(zh, hsel_ref[...]) + _dot(zl, hsel_ref[...])
        return jnp.max(jnp.max(n2, axis=1, keepdims=True), axis=0, keepdims=True)

    sub8 = lax.broadcasted_iota(jnp.int32, (8, LANES), 0)
    st_ref[0] = jnp.where(sub8 == 0, max_head_norm2(q),
                          jnp.where(sub8 == 1, max_head_norm2(k),
                                    jnp.where(sub8 == 2, f_nat[0:1, :],
                                              jnp.where(sub8 == 3, f_nat[tm - 1:tm, :], 0.0))))


def _fox_proj_prompt(x_all, weights, *, t_p, tm, heads):
    d = x_all.shape[1]
    tile = lambda cols: pl.BlockSpec((tm, cols), lambda i: (i, 0))
    sds = lambda cols, dt: jax.ShapeDtypeStruct((t_p, cols), dt)
    n = t_p // tm
    return pl.pallas_call(
        _fox_proj_prompt_body, grid=(n,),
        in_specs=[tile(d)] + [_full_spec(w) for w in weights],
        out_specs=[tile(d)] * 5 + [tile(heads), tile(LANES), pl.BlockSpec((LANES, tm), lambda i: (0, i)),
                                   pl.BlockSpec((1, 8, LANES), lambda i: (i, 0, 0))],
        out_shape=[sds(d, BF16), sds(d, F32), sds(d, F32), sds(d, BF16), sds(d, BF16), sds(heads, F32),
                   sds(LANES, F32), jax.ShapeDtypeStruct((LANES, t_p), F32),
                   jax.ShapeDtypeStruct((n, 8, LANES), F32)],
        scratch_shapes=[pltpu.VMEM((8, LANES), F32), pltpu.VMEM((LANES, LANES), F32)],
        compiler_params=_params(1), name="fox_proj_prompt")(x_all, *weights)


def _fox_attn_prompt_body(start_ref, x_ref, q_ref, k_hbm, v_hbm, f_ref, ft_ref, wo_ref, o_ref,
                          kbuf_ref, vbuf_ref, sem, acc_ref, m_ref, ob_ref, *, heads, hd):
    tq = q_ref.shape[0]
    tk = kbuf_ref.shape[1]
    i = pl.program_id(0)
    j0 = start_ref[i]
    n = i - j0 + 1

    def copies(j, slot):
        rows = pl.ds(pl.multiple_of(j * tk, tk), tk)
        return (pltpu.make_async_copy(k_hbm.at[rows], kbuf_ref.at[slot], sem.at[0, slot]),
                pltpu.make_async_copy(v_hbm.at[rows], vbuf_ref.at[slot], sem.at[1, slot]))

    for cp in copies(j0, 0):
        cp.start()
    acc_ref[...] = jnp.zeros_like(acc_ref)
    m_ref[...] = jnp.full_like(m_ref, NEG)

    lane_k = lax.broadcasted_iota(jnp.int32, (tk, LANES), 1)
    keep_lo = jnp.where(lane_k < hd, 1.0, 0.0).astype(BF16)
    keep_hi = jnp.where(lane_k >= hd, 1.0, 0.0).astype(BF16)
    ones_lo = jnp.where(lane_k == hd, 1.0, 0.0).astype(BF16)
    ones_hi = jnp.where(lane_k == 0, 1.0, 0.0).astype(BF16)

    def block(jj, diagonal):
        j = j0 + jj
        slot = lax.rem(jj, 2)

        @pl.when(jj + 1 < n)
        def _():
            for cp in copies(j + 1, 1 - slot):
                cp.start()

        for cp in copies(j, slot):
            cp.wait()
        if diagonal:
            causal = (lax.broadcasted_iota(jnp.int32, (tq, tk), 1)
                      <= lax.broadcasted_iota(jnp.int32, (tq, tk), 0))
        def scores(h):
            sl = slice((h // 2) * LANES, (h // 2 + 1) * LANES)
            kk = kbuf_ref[slot, :, sl] * (keep_lo if h % 2 == 0 else keep_hi)
            fk = ft_ref[j, h:h + 1, :] - f_ref[0:1, h:h + 1]
            t = _dot_nt(q_ref[:, sl], kk) - fk
            return jnp.where(causal, t, NEG) if diagonal else t

        t_next = scores(0)
        for h in range(heads):
            t = t_next
            if h + 1 < heads:
                t_next = scores(h + 1)
            sl = slice((h // 2) * LANES, (h // 2 + 1) * LANES)
            vp = vbuf_ref[slot, :, sl]
            vv = vp * keep_lo + ones_lo if h % 2 == 0 else vp * keep_hi + ones_hi
            m_prev = m_ref[h]
            m_new = jnp.maximum(m_prev, jnp.max(t, axis=1, keepdims=True))
            alpha = jnp.exp(m_prev - m_new)
            pexp = jnp.exp(t - jnp.tile(m_new, (1, tk // LANES))).astype(BF16)
            acc_ref[h] = alpha * acc_ref[h] + _dot(pexp, vv)
            m_ref[h] = m_new

    def off_diagonal(jj, carry):
        block(jj, False)
        return carry

    lax.fori_loop(0, n - 1, off_diagonal, 0)
    block(n - 1, True)

    lane_q = lax.broadcasted_iota(jnp.int32, (tq, LANES), 1)
    for p in range(heads // 2):
        ae = acc_ref[2 * p]
        ao = acc_ref[2 * p + 1]
        o = jnp.where(lane_q < hd, ae / ae[:, hd:hd + 1], ao / ao[:, 0:1])
        ob_ref[:, p * LANES:(p + 1) * LANES] = o.astype(BF16)
    o_ref[...] = x_ref[...] + _dot(ob_ref[...], wo_ref[...])


SKIP_MARGIN = 100.0


def _fox_band_starts(stats, heads):
    nq = stats.shape[0]
    qmax = jnp.sqrt(stats[:, 0, 0]) * 1.01
    kmax = jnp.sqrt(stats[:, 1, 0]) * 1.01
    f_first = stats[:, 2, :heads]
    f_last = stats[:, 3, :heads]
    gap = jnp.max(f_first[:, None, :] - f_last[None, :, :], axis=2)
    bound = gap + qmax[:, None] * (kmax[None, :] + kmax[:, None]) + 0.01
    tile = jnp.arange(nq, dtype=jnp.int32)
    needed = jnp.logical_or(bound >= -SKIP_MARGIN, tile[None, :] == tile[:, None])
    needed = jnp.logical_and(needed, tile[None, :] <= tile[:, None])
    return jnp.argmax(needed, axis=1).astype(jnp.int32)


def _fox_attn_prompt(x_all, q, kb, vb, f_nat, f_t, stats, w_out, *, t_p, tq, heads):
    d = x_all.shape[1]
    hd = d // heads
    nq = t_p // tq
    starts = _fox_band_starts(stats, heads)
    ft3 = jnp.swapaxes(f_t[:heads].reshape(heads, nq, tq), 0, 1)
    grid_spec = pltpu.PrefetchScalarGridSpec(
        num_scalar_prefetch=1, grid=(nq,),
        in_specs=[
            pl.BlockSpec((tq, d), lambda i, st: (i, 0)),
            pl.BlockSpec((tq, d), lambda i, st: (i, 0)),
            pl.BlockSpec(memory_space=pl.ANY),
            pl.BlockSpec(memory_space=pl.ANY),
            pl.BlockSpec((tq, LANES), lambda i, st: (i, 0)),
            pl.BlockSpec((nq, heads, tq), lambda i, st: (0, 0, 0)),
            pl.BlockSpec((d, d), lambda i, st: (0, 0)),
        ],
        out_specs=pl.BlockSpec((tq, d), lambda i, st: (i, 0)),
        scratch_shapes=[pltpu.VMEM((2, tq, d), BF16), pltpu.VMEM((2, tq, d), BF16),
                        pltpu.SemaphoreType.DMA((2, 2)),
                        pltpu.VMEM((heads, tq, LANES), F32), pltpu.VMEM((heads, tq, LANES), F32),
                        pltpu.VMEM((tq, d), BF16)])
    return pl.pallas_call(
        functools.partial(_fox_attn_prompt_body, heads=heads, hd=hd),
        grid_spec=grid_spec, out_shape=jax.ShapeDtypeStruct(x_all.shape, F32),
        input_output_aliases={1: 0}, compiler_params=_params(1), name="fox_attn_prompt",
    )(starts, x_all, q, kb, vb, f_nat, ft3, w_out)


def _fox_attn_sample_body(q_ref, kn_ref, vn_ref, ck_ref, cv_ref, lft_ref, o_ref,
                          qbd_ref, bias_ref, m_ref, l_ref, acc_ref, *, heads, n_chunks):
    s_new, d = q_ref.shape
    hd = d // heads
    rows = heads * s_new
    kc_len = ck_ref.shape[1]
    past = kc_len * n_chunks
    width = lft_ref.shape[2]
    kc = pl.program_id(1)
    row_h = lax.broadcasted_iota(jnp.int32, (rows, d), 0) // s_new
    head_mask = row_h == lax.broadcasted_iota(jnp.int32, (rows, d), 1) // hd

    @pl.when(kc == 0)
    def _():
        qt = jnp.concatenate([q_ref[...].astype(F32)] * heads, axis=0)
        qbd_ref[...] = jnp.where(head_mask, qt, 0.0).astype(BF16)
        lf = lft_ref[0]
        lane = lax.broadcasted_iota(jnp.int32, (heads, width), 1)
        run = lf
        shift = 1
        while shift < width:
            moved = pltpu.roll(run, width - shift, 1)
            run = run + jnp.where(lane + shift < width, moved, 0.0)
            shift *= 2
        rexcl = run - lf
        rexp = jnp.concatenate(
            [jnp.broadcast_to(rexcl[h:h + 1, :], (s_new, width)) for h in range(heads)], axis=0)
        tail = rexp[:, past:]
        tl = lax.broadcasted_iota(jnp.int32, tail.shape, 1)
        tr = lax.broadcasted_iota(jnp.int32, tail.shape, 0) % s_new
        rq = jnp.sum(jnp.where(tl == tr, tail, 0.0), axis=1, keepdims=True)
        bias_ref[...] = rexp - rq
        m_ref[...] = jnp.full_like(m_ref, NEG)
        l_ref[...] = jnp.zeros_like(l_ref)
        acc_ref[...] = jnp.zeros_like(acc_ref)

    def update(s, vmat):
        m_prev = m_ref[...]
        m_new = jnp.maximum(m_prev, jnp.max(s, axis=1, keepdims=True))
        alpha = jnp.exp(m_prev - m_new)
        p = jnp.exp(s - m_new)
        l_ref[...] = alpha * l_ref[...] + jnp.sum(p, axis=1, keepdims=True)
        acc_ref[...] = alpha * acc_ref[...] + _dot(p.astype(BF16), vmat)
        m_ref[...] = m_new

    for c in range(n_chunks):
        @pl.when(kc == c)
        def _(c=c):
            s = _dot_nt(qbd_ref[...], ck_ref[0].astype(BF16)) + bias_ref[:, c * kc_len:(c + 1) * kc_len]
            update(s, cv_ref[0].astype(BF16))

    @pl.when(kc == n_chunks - 1)
    def _():
        pad = jnp.zeros((width - past - s_new, d), BF16)
        kn = jnp.concatenate([kn_ref[...].astype(BF16), pad], axis=0)
        vn = jnp.concatenate([vn_ref[...].astype(BF16), pad], axis=0)
        s = _dot_nt(qbd_ref[...], kn) + bias_ref[:, past:]
        tl = lax.broadcasted_iota(jnp.int32, s.shape, 1)
        tr = lax.broadcasted_iota(jnp.int32, s.shape, 0) % s_new
        update(jnp.where(tl <= tr, s, NEG), vn)
        o_full = jnp.where(head_mask, acc_ref[...] / l_ref[...], 0.0)
        out = o_full[0:s_new, :]
        for h in range(1, heads):
            out = out + o_full[h * s_new:(h + 1) * s_new, :]
        o_ref[...] = out.astype(BF16)


def _fox_attn_sample(q_s, k_s, v_s, cache_k, cache_v, lft_all, *, heads, s_new):
    t_s, d = q_s.shape
    n_b = t_s // s_new
    past = cache_k.shape[1]
    n_chunks = 1
    kc_len = past // n_chunks
    width = lft_all.shape[2]
    rows = heads * s_new
    return pl.pallas_call(
        functools.partial(_fox_attn_sample_body, heads=heads, n_chunks=n_chunks),
        grid=(n_b, n_chunks),
        in_specs=[
            pl.BlockSpec((s_new, d), lambda b, c: (b, 0)),
            pl.BlockSpec((s_new, d), lambda b, c: (b, 0)),
            pl.BlockSpec((s_new, d), lambda b, c: (b, 0)),
            pl.BlockSpec((1, kc_len, d), lambda b, c: (b, c, 0)),
            pl.BlockSpec((1, kc_len, d), lambda b, c: (b, c, 0)),
            pl.BlockSpec((1, heads, width), lambda b, c: (b, 0, 0)),
        ],
        out_specs=pl.BlockSpec((s_new, d), lambda b, c: (b, 0)),
        out_shape=jax.ShapeDtypeStruct((t_s, d), BF16),
        scratch_shapes=[pltpu.VMEM((rows, d), BF16), pltpu.VMEM((rows, width), F32),
                        pltpu.VMEM((rows, 1), F32), pltpu.VMEM((rows, 1), F32),
                        pltpu.VMEM((rows, d), F32)],
        compiler_params=_params(2), name="fox_attn_sample")(q_s, k_s, v_s, cache_k, cache_v, lft_all)


def _proj_residual_body(x_ref, o_ref_in, w_ref, o_ref):
    o_ref[...] = x_ref[...] + _dot(o_ref_in[...], w_ref[...])


def _sgu_body(x_ref, g_ref, win_ref, lg_ref, lb_ref, ws_ref, bs_ref, wo_ref, o_ref, v_ref, mix_ref,
              *, n_prompt_tiles, groups):
    tm, d = x_ref.shape
    gc = d // groups
    i = pl.program_id(0)
    sel = (i >= n_prompt_tiles).astype(jnp.int32)
    hn = _rms(x_ref[...], g_ref[...]).astype(BF16)
    z = _dot(hn, win_ref[...])
    z = 0.5 * z * (1.0 + lax.erf(z * (2.0 ** -0.5)))
    u = z[:, :d]
    v = _ln(z[:, d:], lg_ref[...], lb_ref[...])
    v_ref[...] = v
    vb = v.astype(BF16)
    for c in range(tm // SGU_CHUNK):
        rs = slice(c * SGU_CHUNK, (c + 1) * SGU_CHUNK)
        for g in range(groups):
            cs = slice(g * gc, (g + 1) * gc)
            mix_ref[rs, cs] = _dot(ws_ref[sel, g], vb[rs, cs])
        mix_ref[rs, :] = mix_ref[rs, :] + bs_ref[sel]
    o_ref[...] = x_ref[...] + _dot((u * mix_ref[...]).astype(BF16), wo_ref[...])


def _xattn_core(q, k, v, heads):
    d = q.shape[1]
    hd = d // heads
    outs = []
    for h in range(heads):
        sl = slice(h * hd, (h + 1) * hd)
        s = _dot_nt(q[:, sl], k[:, sl])
        p = jnp.exp(s - jnp.max(s, axis=1, keepdims=True))
        p = p / jnp.sum(p, axis=1, keepdims=True)
        outs.append(_dot(p.astype(BF16), v[:, sl]).astype(BF16))
    return outs


def _xattn_prompt_body(x_ref, g_ref, wq_ref, k_ref, v_ref, wo_ref, o_ref, *, heads, layer):
    d = x_ref.shape[1]
    hd = d // heads
    x = x_ref[...]
    q = _dot(_rms(x, g_ref[...]).astype(BF16), wq_ref[...]).astype(BF16)
    k = k_ref[layer].astype(BF16)
    v = v_ref[layer].astype(BF16)
    y = x
    for h, oh in enumerate(_xattn_core(q, k, v, heads)):
        y = y + _dot(oh, wo_ref[h * hd:(h + 1) * hd, :])
    o_ref[...] = y


def _xattn_sample_body(x_ref, g_ref, wq_ref, k_ref, v_ref, wo_ref, o_ref, ob_ref, *, heads, s_new):
    d = x_ref.shape[1]
    hd = d // heads
    bb = k_ref.shape[1]
    x = x_ref[...]
    q = _dot(_rms(x, g_ref[...]).astype(BF16), wq_ref[...]).astype(BF16)
    for b in range(bb):
        rs = slice(b * s_new, (b + 1) * s_new)
        outs = _xattn_core(q[rs, :], k_ref[0, b].astype(BF16), v_ref[0, b].astype(BF16), heads)
        for h, oh in enumerate(outs):
            ob_ref[rs, h * hd:(h + 1) * hd] = oh
    o_ref[...] = x + _dot(ob_ref[...], wo_ref[...])


def _ffn_body(x_ref, g_ref, wg_ref, wu_ref, wd_ref, o_ref, hn_ref, acc_ref):
    f = pl.program_id(1)

    @pl.when(f == 0)
    def _():
        hn_ref[...] = _rms(x_ref[...], g_ref[...]).astype(BF16)
        acc_ref[...] = jnp.zeros_like(acc_ref)

    hn = hn_ref[...]
    a = _silu(_dot(hn, wg_ref[...])) * _dot(hn, wu_ref[...])
    acc_ref[...] += _dot(a.astype(BF16), wd_ref[...])

    @pl.when(f == pl.num_programs(1) - 1)
    def _():
        o_ref[...] = x_ref[...] + acc_ref[...]


def _ffn(x_all, g, w_gu, w_down, *, tm, tf):
    m, d = x_all.shape
    ff = w_down.shape[0]
    nf = ff // tf
    return pl.pallas_call(
        _ffn_body, grid=(m // tm, nf),
        in_specs=[pl.BlockSpec((tm, d), lambda i, f: (i, 0)),
                  _full_spec(g),
                  pl.BlockSpec((d, tf), lambda i, f: (0, f)),
                  pl.BlockSpec((d, tf), lambda i, f, nf=nf: (0, nf + f)),
                  pl.BlockSpec((tf, d), lambda i, f: (f, 0))],
        out_specs=pl.BlockSpec((tm, d), lambda i, f: (i, 0)),
        out_shape=jax.ShapeDtypeStruct((m, d), F32),
        input_output_aliases={0: 0},
        scratch_shapes=[pltpu.VMEM((tm, d), BF16), pltpu.VMEM((tm, d), F32)],
        compiler_params=_params(2), name="ffn")(x_all, g, w_gu, w_gu, w_down)


def _router_body(x_ref, g_ref, rh_ref, rl_ref, idx_ref, gate_ref, rank_ref, cnt_ref, carry_ref, *, n_experts):
    tm = x_ref.shape[0]

    @pl.when(pl.program_id(0) == 0)
    def _():
        carry_ref[...] = jnp.zeros_like(carry_ref)

    hn32 = _rms(x_ref[...], g_ref[...])
    logits = _dot_hi(hn32, rh_ref[...], rl_ref[...])
    lane = lax.broadcasted_iota(jnp.int32, logits.shape, 1)
    lane_f = lane.astype(F32)
    big = float(LANES)
    masked = jnp.where(lane < n_experts, logits, -jnp.inf)
    m1 = jnp.max(masked, axis=1, keepdims=True)
    i1 = jnp.min(jnp.where(masked == m1, lane_f, big), axis=1, keepdims=True)
    masked2 = jnp.where(lane_f == i1, -jnp.inf, masked)
    m2 = jnp.max(masked2, axis=1, keepdims=True)
    i2 = jnp.min(jnp.where(masked2 == m2, lane_f, big), axis=1, keepdims=True)
    e = jnp.exp(m2 - m1)
    w1 = 1.0 / (1.0 + e)
    w2 = e * w1
    idx_ref[...] = jnp.where(lane == 0, i1, jnp.where(lane == 1, i2, 0.0)).astype(jnp.int32)
    gate_ref[...] = jnp.where(lane == 0, w1, jnp.where(lane == 1, w2, 0.0))
    hit1 = lane_f == i1
    hit2 = lane_f == i2
    onehot = jnp.where(jnp.logical_or(hit1, hit2), 1.0, 0.0).astype(BF16)
    r = lax.broadcasted_iota(jnp.int32, (tm, tm), 0)
    c = lax.broadcasted_iota(jnp.int32, (tm, tm), 1)
    before = jnp.where(c < r, 1.0, 0.0).astype(BF16)
    base = _dot(before, onehot) + carry_ref[0:1, :]
    r1 = jnp.sum(jnp.where(hit1, base, 0.0), axis=1, keepdims=True)
    r2 = jnp.sum(jnp.where(hit2, base, 0.0), axis=1, keepdims=True)
    rank_ref[...] = jnp.where(lane == 0, r1, jnp.where(lane == 1, r2, 0.0)).astype(jnp.int32)
    carry_ref[...] = carry_ref[...] + _dot(jnp.ones((8, tm), BF16), onehot)
    cnt_ref[...] = carry_ref[...]


def _moe_router(x_all, g, r_hi, r_lo, *, tm, n_experts):
    m, d = x_all.shape
    tile = pl.BlockSpec((tm, LANES), lambda i: (i, 0))
    return pl.pallas_call(
        functools.partial(_router_body, n_experts=n_experts), grid=(m // tm,),
        in_specs=[pl.BlockSpec((tm, d), lambda i: (i, 0)), _full_spec(g), _full_spec(r_hi), _full_spec(r_lo)],
        out_specs=[tile, tile, tile, pl.BlockSpec((8, LANES), lambda i: (0, 0))],
        out_shape=[jax.ShapeDtypeStruct((m, LANES), jnp.int32), jax.ShapeDtypeStruct((m, LANES), F32),
                   jax.ShapeDtypeStruct((m, LANES), jnp.int32), jax.ShapeDtypeStruct((8, LANES), F32)],
        scratch_shapes=[pltpu.VMEM((8, LANES), F32)],
        compiler_params=_params(1), name="moe_router")(x_all, g, r_hi, r_lo)


def _row_copies(src_ref, dst_ref, idx_ref, base, sem, n, *, gather, stride=1, offset=0):
    def copy(j):
        row = idx_ref[base + j * stride + offset]
        if gather:
            return pltpu.make_async_copy(src_ref.at[pl.ds(row, 1)], dst_ref.at[pl.ds(j, 1)], sem)
        return pltpu.make_async_copy(src_ref.at[pl.ds(j, 1)], dst_ref.at[pl.ds(row, 1)], sem)

    def issue(j, c):
        copy(j).start()
        return c

    def wait(j, c):
        copy(j).wait()
        return c

    lax.fori_loop(0, n, issue, 0, unroll=8)
    lax.fori_loop(0, n, wait, 0, unroll=8)


def _moe_dispatch_body(dest_ref, x_ref, g_ref, init_hbm, out_hbm, hbuf_ref, sem0, sem1):
    del init_hbm
    tm = x_ref.shape[0]
    base = pl.program_id(0) * (tm * TOP_K)
    hbuf_ref[...] = _rms(x_ref[...], g_ref[...])
    _row_copies(hbuf_ref, out_hbm, dest_ref, base, sem0, tm, gather=False, stride=TOP_K, offset=0)
    _row_copies(hbuf_ref, out_hbm, dest_ref, base, sem1, tm, gather=False, stride=TOP_K, offset=1)


def _moe_dispatch(x_all, g, dest, *, tm, rows_sorted):
    m, d = x_all.shape
    grid_spec = pltpu.PrefetchScalarGridSpec(
        num_scalar_prefetch=1, grid=(m // tm,),
        in_specs=[pl.BlockSpec((tm, d), lambda i, dest: (i, 0)),
                  pl.BlockSpec((1, d), lambda i, dest: (0, 0)),
                  pl.BlockSpec(memory_space=pl.ANY)],
        out_specs=pl.BlockSpec(memory_space=pl.ANY),
        scratch_shapes=[pltpu.VMEM((tm, d), F32), pltpu.SemaphoreType.DMA(()), pltpu.SemaphoreType.DMA(())])
    return pl.pallas_call(
        _moe_dispatch_body, grid_spec=grid_spec, out_shape=jax.ShapeDtypeStruct((rows_sorted, d), F32),
        input_output_aliases={3: 0}, compiler_params=_params(1), name="moe_dispatch",
    )(dest, x_all, g, jnp.zeros((rows_sorted, d), F32))


def _moe_ffn_body(te_ref, nt_ref, hs_ref, wg_ref, wu_ref, wd_ref, o_ref, hb_ref, acc_ref):
    r = pl.program_id(0)
    f = pl.program_id(1)
    valid = r < nt_ref[0]

    @pl.when(jnp.logical_and(valid, f == 0))
    def _():
        hb_ref[...] = hs_ref[...].astype(BF16)
        acc_ref[...] = jnp.zeros_like(acc_ref)

    @pl.when(valid)
    def _():
        h = hb_ref[...]
        a = _silu(_dot(h, wg_ref[0, 0])) * _dot(h, wu_ref[0, 0])
        acc_ref[...] += _dot(a.astype(BF16), wd_ref[0, 0])

    last = f == pl.num_programs(1) - 1

    @pl.when(jnp.logical_and(valid, last))
    def _():
        o_ref[...] = acc_ref[...]

    @pl.when(jnp.logical_and(jnp.logical_not(valid), last))
    def _():
        o_ref[...] = jnp.zeros_like(o_ref)


def _moe_ffn(hn_sorted, tile_expert, n_tiles, w_gu, w_down, *, layer, tm, tf):
    rows, d = hn_sorted.shape
    d_exp = w_down.shape[2]
    nf = d_exp // tf

    def fe(r, f, nt):
        return jnp.where(r < nt[0], f, nf - 1)

    grid_spec = pltpu.PrefetchScalarGridSpec(
        num_scalar_prefetch=2, grid=(rows // tm, nf),
        in_specs=[
            pl.BlockSpec((tm, d), lambda r, f, te, nt: (r, 0)),
            pl.BlockSpec((1, 1, d, tf), lambda r, f, te, nt: (layer, te[r], 0, fe(r, f, nt))),
            pl.BlockSpec((1, 1, d, tf), lambda r, f, te, nt: (layer, te[r], 0, nf + fe(r, f, nt))),
            pl.BlockSpec((1, 1, tf, d), lambda r, f, te, nt: (layer, te[r], fe(r, f, nt), 0)),
        ],
        out_specs=pl.BlockSpec((tm, d), lambda r, f, te, nt: (r, 0)),
        scratch_shapes=[pltpu.VMEM((tm, d), BF16), pltpu.VMEM((tm, d), F32)])
    return pl.pallas_call(
        _moe_ffn_body, grid_spec=grid_spec, out_shape=jax.ShapeDtypeStruct((rows, d), F32),
        compiler_params=_params(2), name="moe_ffn")(tile_expert, n_tiles, hn_sorted, w_gu, w_gu, w_down)


def _moe_combine_body(dest_ref, x_ref, gate_ref, y_hbm, o_ref, b0_ref, b1_ref, sem0, sem1):
    tm = x_ref.shape[0]
    base = pl.program_id(0) * (tm * TOP_K)
    _row_copies(y_hbm, b0_ref, dest_ref, base, sem0, tm, gather=True, stride=TOP_K, offset=0)
    _row_copies(y_hbm, b1_ref, dest_ref, base, sem1, tm, gather=True, stride=TOP_K, offset=1)
    gate = gate_ref[...]
    o_ref[...] = x_ref[...] + gate[:, 0:1] * b0_ref[...] + gate[:, 1:2] * b1_ref[...]


def _moe_combine(x_all, gates, dest, y_sorted, *, tm):
    m, d = x_all.shape
    grid_spec = pltpu.PrefetchScalarGridSpec(
        num_scalar_prefetch=1, grid=(m // tm,),
        in_specs=[pl.BlockSpec((tm, d), lambda i, dest: (i, 0)),
                  pl.BlockSpec((tm, LANES), lambda i, dest: (i, 0)),
                  pl.BlockSpec(memory_space=pl.ANY)],
        out_specs=pl.BlockSpec((tm, d), lambda i, dest: (i, 0)),
        scratch_shapes=[pltpu.VMEM((tm, d), F32), pltpu.VMEM((tm, d), F32),
                        pltpu.SemaphoreType.DMA(()), pltpu.SemaphoreType.DMA(())])
    return pl.pallas_call(
        _moe_combine_body, grid_spec=grid_spec, out_shape=jax.ShapeDtypeStruct((m, d), F32),
        input_output_aliases={1: 0}, compiler_params=_params(1), name="moe_combine",
    )(dest, x_all, gates, y_sorted)


def _moe_plan(top_i, rank, counts, tm, r_max):
    n_experts = counts.shape[0]
    tiles = (counts + tm - 1) // tm
    tile_end = jnp.cumsum(tiles)
    row_start = (tile_end - tiles) * tm
    onehot = top_i[:, :, None] == jnp.arange(n_experts, dtype=jnp.int32)[None, None, :]
    dest = (jnp.sum(jnp.where(onehot, row_start[None, None, :], 0), axis=2) + rank).astype(jnp.int32)
    n_tiles = tile_end[-1:].astype(jnp.int32)
    tile_id = jnp.arange(r_max, dtype=jnp.int32)
    tile_expert = jnp.minimum(jnp.sum((tile_id[:, None] >= tile_end[None, :]).astype(jnp.int32), axis=1),
                              n_experts - 1).astype(jnp.int32)
    return dest.reshape(-1), n_tiles, tile_expert


def _final_norm_body(x_ref, g_ref, o_ref):
    o_ref[...] = _rms(x_ref[...], g_ref[...])


def _pick(n, cands):
    for c in cands:
        if n % c == 0:
            return c
    raise ValueError(f"no tile in {cands} divides {n}")


def kernel(x_prompt, x_sample, mem_prompt, cache_mem_k, cache_mem_v, state_conv, cache_fox_k, cache_fox_v, cache_fox_logf, norm_mix, norm_mem_q, norm_ffn, norm_mem, norm_final, conv_w_in, conv_dw, conv_dw_b, conv_ln_g, conv_ln_b, conv_w_out, fox_w_in, fox_b_f, fox_w_out, sgu_w_in, sgu_ln_g, sgu_ln_b, sgu_w_s, sgu_b_s, sgu_w_out, xa_w_q, xa_w_kv, xa_w_o, ffn_w_gu, ffn_w_down, moe_router, moe_w_gu, moe_w_down):
    batch, t_p, d = x_prompt.shape
    n_b, s_new, _ = x_sample.shape
    assert batch == 1
    t_s = n_b * s_new
    m = t_p + t_s
    depth = norm_mix.shape[0]
    n_mem = mem_prompt.shape[1]
    xa_heads = cache_mem_k.shape[3]
    taps = conv_dw.shape[1]
    fox_heads = fox_b_f.shape[1]
    past = cache_fox_k.shape[2]
    groups = sgu_w_s.shape[1]
    n_experts = moe_router.shape[2]
    d_exp = moe_w_down.shape[2]
    d_ff = ffn_w_down.shape[1]

    tm = _pick(int(np.gcd(t_p, t_s)), (512, 256, 128))
    n_pt = t_p // tm
    n_st = t_s // tm
    n_t = n_pt + n_st
    halo = 32
    assert taps - 1 <= halo and tm % halo == 0 and tm % SGU_CHUNK == 0 and SGU_CHUNK % s_new == 0
    assert fox_heads % 2 == 0 and d // fox_heads * 2 == LANES and fox_heads <= LANES
    assert past % SGU_CHUNK == 0 and s_new <= SGU_BLOCK

    row = lambda a: a.reshape(1, -1)
    x_all = jnp.concatenate([x_prompt[0], x_sample.reshape(t_s, d)], axis=0)

    mem_k, mem_v = _mem_kv(mem_prompt[0], row(norm_mem), xa_w_kv.astype(BF16))
    moe_w_gu_b = moe_w_gu.astype(BF16)
    moe_w_down_b = moe_w_down.astype(BF16)
    cache_k4 = cache_mem_k.astype(BF16).reshape(depth, n_b, n_mem, d)
    cache_v4 = cache_mem_v.astype(BF16).reshape(depth, n_b, n_mem, d)

    conv_p, conv_s = [], []
    fox_out = []
    sgu_v = []
    for i in range(depth):
        kind = i % 3
        j = i // 3
        g_mix = row(norm_mix[i])
        if kind == 0:
            (u_all,) = _row_call(
                _conv_in_body, grid_tiles=n_t, tm=tm, first_tile=0, row_ins=[x_all],
                full_ins=[g_mix, conv_w_in[j].astype(BF16)], outs=[(m, d, F32, 0)], alias_x=False,
                name="conv_in")
            dw = jnp.concatenate([conv_dw[j], jnp.zeros((halo - taps, d), F32)], axis=0)
            tail_w = [dw, row(conv_dw_b[j]), row(conv_ln_g[j]), row(conv_ln_b[j]), conv_w_out[j].astype(BF16)]
            hb = tm // halo
            prompt_w = [jnp.broadcast_to(conv_dw[j][:, None, :], (taps, 8, d))] + tail_w[1:]
            x_all = pl.pallas_call(
                functools.partial(_conv_prompt_body, taps=taps, halo=halo, rc=32), grid=(n_pt,),
                in_specs=[pl.BlockSpec((tm, d), lambda t: (t, 0)),
                          pl.BlockSpec((tm, d), lambda t: (t, 0)),
                          pl.BlockSpec((halo, d), lambda t, hb=hb: (jnp.maximum(t * hb - 1, 0), 0))]
                         + [_full_spec(a) for a in prompt_w],
                out_specs=pl.BlockSpec((tm, d), lambda t: (t, 0)),
                out_shape=jax.ShapeDtypeStruct((m, d), F32), input_output_aliases={0: 0},
                scratch_shapes=[pltpu.VMEM((tm + halo + 8, d), F32),
                                pltpu.VMEM((8, tm + 8 * ((taps - 1) // 8), d), F32),
                                pltpu.VMEM((tm, d), F32)],
                compiler_params=_params(1), name="conv_prompt")(x_all, u_all, u_all, *prompt_w)
            u_s = u_all[t_p:].reshape(n_b, s_new, d)
            ext_s = jnp.concatenate([state_conv[j], u_s], axis=1)
            bb = _pick(n_b, (8, 4, 2, 1))
            rows_s = bb * s_new
            x_all = pl.pallas_call(
                functools.partial(_conv_sample_body, taps=taps), grid=(n_b // bb,),
                in_specs=[pl.BlockSpec((rows_s, d), lambda t, o=t_p // rows_s: (t + o, 0)),
                          pl.BlockSpec((bb, taps - 1 + s_new, d), lambda t: (t, 0, 0))]
                         + [_full_spec(a) for a in tail_w],
                out_specs=pl.BlockSpec((rows_s, d), lambda t, o=t_p // rows_s: (t + o, 0)),
                out_shape=jax.ShapeDtypeStruct((m, d), F32), input_output_aliases={0: 0},
                scratch_shapes=[pltpu.VMEM((rows_s, d), F32)],
                compiler_params=_params(1), name="conv_sample")(x_all, ext_s, *tail_w)
            conv_p.append(u_all[t_p - (taps - 1):t_p][None])
            conv_s.append(ext_s[:, s_new:])
        elif kind == 1:
            hd = d // fox_heads
            w_in = fox_w_in[j]
            w_qkv = jnp.concatenate([w_in[:, :d] * (hd ** -0.5), w_in[:, d:3 * d]], axis=1).astype(BF16)
            w_f = jnp.concatenate([w_in[:, 3 * d:], jnp.zeros((d, LANES - fox_heads), F32)], axis=1)
            wf_hi = w_f.astype(BF16)
            wf_lo = (w_f - wf_hi.astype(F32)).astype(BF16)
            b_f = jnp.concatenate([fox_b_f[j], jnp.zeros((LANES - fox_heads,), F32)])
            b_row = row(b_f)
            ta = min(tm, FOX_TILE)
            b_col = jnp.broadcast_to(b_f[:, None], (LANES, ta))
            head_sel = (jnp.arange(d)[:, None] // hd == jnp.arange(LANES)[None, :]).astype(BF16)
            proj_w = [g_mix, w_qkv, wf_hi, wf_lo, b_row]
            q_p, k_p, v_p, kb_p, vb_p, lf_p, f_nat, f_t, stats = _fox_proj_prompt(
                x_all, proj_w + [wf_hi.T, wf_lo.T, b_col, head_sel], t_p=t_p, tm=ta, heads=fox_heads)
            w_out = fox_w_out[j].astype(BF16)
            x_all = _fox_attn_prompt(x_all, q_p, kb_p, vb_p, f_nat, f_t, stats, w_out, t_p=t_p, tq=ta,
                                     heads=fox_heads)
            q_s, k_s, v_s, lf_s = _row_call(
                _fox_proj_sample_body, grid_tiles=n_st, tm=tm, first_tile=n_pt, row_ins=[x_all],
                full_ins=proj_w,
                outs=[(t_s, d, BF16, 0), (t_s, d, F32, 0), (t_s, d, F32, 0), (t_s, fox_heads, F32, 0)],
                alias_x=False, name="fox_proj_sample")
            width = past + LANES
            lft_all = jnp.concatenate(
                [jnp.swapaxes(cache_fox_logf[j], 1, 2),
                 jnp.swapaxes(lf_s.reshape(n_b, s_new, fox_heads), 1, 2),
                 jnp.zeros((n_b, fox_heads, width - past - s_new), F32)], axis=2)
            o_s = _fox_attn_sample(q_s, k_s, v_s, cache_fox_k[j].astype(BF16).reshape(n_b, past, d),
                                   cache_fox_v[j].astype(BF16).reshape(n_b, past, d), lft_all,
                                   heads=fox_heads, s_new=s_new)
            x_all = pl.pallas_call(
                _proj_residual_body, grid=(n_st,),
                in_specs=[pl.BlockSpec((tm, d), lambda t, o=n_pt: (t + o, 0)),
                          pl.BlockSpec((tm, d), lambda t: (t, 0)), _full_spec(w_out)],
                out_specs=pl.BlockSpec((tm, d), lambda t, o=n_pt: (t + o, 0)),
                out_shape=jax.ShapeDtypeStruct((m, d), F32), input_output_aliases={0: 0},
                compiler_params=_params(1), name="fox_out_sample")(x_all, o_s, w_out)
            fox_out.append((k_p, v_p, lf_p, k_s, v_s, lf_s))
        else:
            ws = sgu_w_s[j]
            pos = jnp.arange(SGU_CHUNK)
            blk_mask = (pos[None, :] // SGU_BLOCK) <= (pos[:, None] // SGU_BLOCK)
            ws_p = jnp.where(blk_mask[None], ws, 0.0)
            reps = SGU_CHUNK // s_new
            eye = jnp.eye(reps, dtype=F32)
            ws_s = jnp.einsum("ab,gij->gaibj", eye, ws_p[:, :s_new, :s_new]).reshape(
                groups, SGU_CHUNK, SGU_CHUNK)
            ws_all = jnp.stack([ws_p, ws_s]).astype(BF16)
            gc = d // groups
            bs_p = jnp.repeat(sgu_b_s[j].T, gc, axis=1)
            bs_s = jnp.tile(bs_p[:s_new], (reps, 1))
            bs_all = jnp.stack([bs_p, bs_s])
            x_all, v_all = _row_call(
                functools.partial(_sgu_body, n_prompt_tiles=n_pt, groups=groups),
                grid_tiles=n_t, tm=tm, first_tile=0, row_ins=[x_all],
                full_ins=[g_mix, sgu_w_in[j].astype(BF16), row(sgu_ln_g[j]), row(sgu_ln_b[j]),
                          ws_all, bs_all, sgu_w_out[j].astype(BF16)],
                outs=[(m, d, F32, 0), (m, d, F32, 0)], scratch=[pltpu.VMEM((tm, d), F32)], name="sgu")
            sgu_v.append(v_all[t_p:].reshape(n_b, s_new, d))

        xa_hd = d // xa_heads
        wq = (xa_w_q[i] * (xa_hd ** -0.5)).astype(BF16)
        wo = xa_w_o[i].astype(BF16)
        g_q = row(norm_mem_q[i])
        (x_all,) = _row_call(
            functools.partial(_xattn_prompt_body, heads=xa_heads, layer=i),
            grid_tiles=n_pt, tm=tm, first_tile=0, row_ins=[x_all],
            full_ins=[g_q, wq, mem_k, mem_v, wo], outs=[(m, d, F32, 0)], name="xattn_prompt")
        bb = _pick(n_b, (4, 2, 1))
        rows_s = bb * s_new
        x_all = pl.pallas_call(
            functools.partial(_xattn_sample_body, heads=xa_heads, s_new=s_new), grid=(n_b // bb,),
            in_specs=[pl.BlockSpec((rows_s, d), lambda t, o=t_p // rows_s: (t + o, 0)),
                      _full_spec(g_q), _full_spec(wq),
                      pl.BlockSpec((1, bb, n_mem, d), lambda t, i=i: (i, t, 0, 0)),
                      pl.BlockSpec((1, bb, n_mem, d), lambda t, i=i: (i, t, 0, 0)),
                      _full_spec(wo)],
            out_specs=pl.BlockSpec((rows_s, d), lambda t, o=t_p // rows_s: (t + o, 0)),
            out_shape=jax.ShapeDtypeStruct((m, d), F32), input_output_aliases={0: 0},
            scratch_shapes=[pltpu.VMEM((rows_s, d), BF16)],
            compiler_params=_params(1), name="xattn_sample")(x_all, g_q, wq, cache_k4, cache_v4, wo)

        g_f = row(norm_ffn[i])
        c = i // 2
        if i % 2 == 0:
            tf = _pick(d_ff, (1408, 1024, 512, 256, 128))
            x_all = _ffn(x_all, g_f, ffn_w_gu[c].astype(BF16), ffn_w_down[c].astype(BF16), tm=tm, tf=tf)
        else:
            router = jnp.concatenate([moe_router[c], jnp.zeros((d, LANES - n_experts), F32)], axis=1)
            r_hi = router.astype(BF16)
            r_lo = (router - r_hi.astype(F32)).astype(BF16)
            idx, gates, rank, counts = _moe_router(x_all, g_f, r_hi, r_lo, tm=tm, n_experts=n_experts)
            r_max = (TOP_K * m + n_experts * (tm - 1)) // tm
            dest, n_tiles, tile_expert = _moe_plan(idx[:, :TOP_K], rank[:, :TOP_K],
                                                   counts[0, :n_experts].astype(jnp.int32), tm, r_max)
            hn_sorted = _moe_dispatch(x_all, g_f, dest, tm=tm, rows_sorted=r_max * tm)
            tf = _pick(d_exp, (1792, 1024, 512, 256, 128))
            y_sorted = _moe_ffn(hn_sorted, tile_expert, n_tiles, moe_w_gu_b, moe_w_down_b, layer=c, tm=tm, tf=tf)
            x_all = _moe_combine(x_all, gates, dest, y_sorted, tm=tm)

    (y_all,) = _row_call(_final_norm_body, grid_tiles=n_t, tm=tm, first_tile=0, row_ins=[x_all],
                         full_ins=[row(norm_final)], outs=[(m, d, F32, 0)], alias_x=False,
                         name="final_norm")
    y_prompt = y_all[:t_p][None]
    y_sample = y_all[t_p:].reshape(n_b, s_new, d)
    mem_shape = (depth, 1, n_mem, xa_heads, d // xa_heads)
    fk_p, fv_p, fl_p, fk_s, fv_s, fl_s = zip(*fox_out)
    hshape_p = (1, t_p, fox_heads, d // fox_heads)
    hshape_s = (n_b, s_new, fox_heads, d // fox_heads)
    return (y_prompt, y_sample, mem_k.reshape(mem_shape), mem_v.reshape(mem_shape),
            jnp.stack(conv_p), jnp.stack(conv_s),
            jnp.stack([a.reshape(hshape_p) for a in fk_p]), jnp.stack([a.reshape(hshape_p) for a in fv_p]),
            jnp.stack([a.reshape(1, t_p, fox_heads) for a in fl_p]),
            jnp.stack([a.reshape(hshape_s) for a in fk_s]), jnp.stack([a.reshape(hshape_s) for a in fv_s]),
            jnp.stack([a.reshape(n_b, s_new, fox_heads) for a in fl_s]),
            jnp.stack(sgu_v))
```

```python
import functools

import numpy as np
import jax
import jax.numpy as jnp
from jax import lax
from jax.experimental import pallas as pl
from jax.experimental.pallas import tpu as pltpu

F32 = jnp.float32
BF16 = jnp.bfloat16
RMS_EPS = 1e-6
LN_EPS = 1e-5
SGU_CHUNK = 128
SGU_BLOCK = 64
TOP_K = 2
LANES = 128
NEG = -1e30
LOG2E = 1.4426950408889634
VMEM_LIMIT = 52 * 1024 * 1024
FOX_TILE = 256


def _params(n_axes, vmem=VMEM_LIMIT):
    return pltpu.CompilerParams(dimension_semantics=("arbitrary",) * n_axes, vmem_limit_bytes=vmem)


def _dot(a, b):
    return jnp.dot(a, b, preferred_element_type=F32)


def _dot_nt(a, b):
    return lax.dot_general(a, b, (((1,), (1,)), ((), ())), preferred_element_type=F32)


def _rms(x, g):
    return x * lax.rsqrt(jnp.mean(x * x, axis=-1, keepdims=True) + RMS_EPS) * g


def _ln(x, g, b):
    mu = jnp.mean(x, axis=-1, keepdims=True)
    xc = x - mu
    var = jnp.mean(xc * xc, axis=-1, keepdims=True)
    return xc * lax.rsqrt(var + LN_EPS) * g + b


def _silu(x):
    return x * jax.nn.sigmoid(x)


def _log_sigmoid(x):
    return jnp.minimum(x, 0.0) - jnp.log(1.0 + jnp.exp(-jnp.abs(x)))


def _split3(x):
    a = x.astype(BF16)
    r = x - a.astype(F32)
    b = r.astype(BF16)
    c = (r - b.astype(F32)).astype(BF16)
    return a, b, c


def _split2(x):
    a = x.astype(BF16)
    return a, (x - a.astype(F32)).astype(BF16)


def _dot_hi(x32, w_hi, w_lo):
    xh, xl = _split2(x32)
    return _dot(xh, w_hi) + _dot(xl, w_hi) + _dot(xh, w_lo)


def _full_spec(a):
    nd = a.ndim
    return pl.BlockSpec(a.shape, lambda *_, nd=nd: (0,) * nd)


def _row_call(body, *, grid_tiles, tm, first_tile, row_ins, full_ins, outs, alias_x=True,
              scratch=(), name=None):
    in_specs = [pl.BlockSpec((tm, a.shape[1]), lambda i, ft=first_tile: (i + ft, 0)) for a in row_ins]
    in_specs += [_full_spec(a) for a in full_ins]
    out_shape, out_specs = [], []
    for rows, cols, dt, off in outs:
        out_shape.append(jax.ShapeDtypeStruct((rows, cols), dt))
        out_specs.append(pl.BlockSpec((tm, cols), lambda i, off=off: (i + off, 0)))
    return pl.pallas_call(
        body, grid=(grid_tiles,), in_specs=in_specs, out_specs=out_specs, out_shape=out_shape,
        input_output_aliases={0: 0} if alias_x else {}, scratch_shapes=list(scratch),
        compiler_params=_params(1), name=name)(*row_ins, *full_ins)


def _mem_kv_body(mem_ref, g_ref, w_ref, k_ref, v_ref):
    d = mem_ref.shape[1]
    mn = _rms(mem_ref[...], g_ref[...]).astype(BF16)
    kv = _dot(mn, w_ref[0])
    k_ref[0] = kv[:, :d]
    v_ref[0] = kv[:, d:]


def _mem_kv(mem, g, w_kv):
    depth, d, _ = w_kv.shape
    n_mem = mem.shape[0]
    return pl.pallas_call(
        _mem_kv_body, grid=(depth,),
        in_specs=[_full_spec(mem), _full_spec(g), pl.BlockSpec((1, d, 2 * d), lambda i: (i, 0, 0))],
        out_specs=[pl.BlockSpec((1, n_mem, d), lambda i: (i, 0, 0))] * 2,
        out_shape=[jax.ShapeDtypeStruct((depth, n_mem, d), F32)] * 2,
        compiler_params=_params(1), name="mem_kv")(mem, g, w_kv)


def _conv_in_body(x_ref, g_ref, w_ref, u_ref):
    d = u_ref.shape[1]
    hn = _rms(x_ref[...], g_ref[...]).astype(BF16)
    z = _dot(hn, w_ref[...])
    u_ref[...] = z[:, :d] * jax.nn.sigmoid(z[:, d:])


def _conv_tail(c, lg, lb, wo):
    c = _silu(_ln(c, lg, lb))
    return _dot(c.astype(BF16), wo)


def _conv_prompt_body(x_ref, u_ref, halo_ref, dw8_ref, dwb_ref, lg_ref, lb_ref, wo_ref, o_ref,
                      ext_ref, sh_ref, c_ref, *, taps, halo, rc):
    tm, d = u_ref.shape
    i = pl.program_id(0)
    ext_ref[0:halo, :] = jnp.where(i == 0, 0.0, halo_ref[...])
    ext_ref[halo:halo + tm, :] = u_ref[...]
    ext_ref[halo + tm:, :] = jnp.zeros((ext_ref.shape[0] - halo - tm, d), F32)
    off = halo - (taps - 1)
    n_sh = sh_ref.shape[1]
    for b in range(8):
        sh_ref[b] = ext_ref[pl.ds(off + b, n_sh), :]

    span = rc + 8 * ((taps - 1) // 8)
    lw = min(d, 256)

    def chunk(r, carry):
        r0 = pl.multiple_of(r * rc, rc)
        for l0 in range(0, d, lw):
            acc = jnp.zeros((rc, lw), F32) + dwb_ref[:, l0:l0 + lw]
            for b in range(8):
                win = sh_ref[b, pl.ds(r0, span), l0:l0 + lw]
                for a in range((taps - 1 - b) // 8 + 1):
                    w = jnp.concatenate([dw8_ref[8 * a + b, :, l0:l0 + lw]] * (rc // 8), axis=0)
                    acc = acc + win[8 * a:8 * a + rc] * w
            c_ref[pl.ds(r0, rc), l0:l0 + lw] = acc
        return carry

    lax.fori_loop(0, tm // rc, chunk, 0)
    o_ref[...] = x_ref[...] + _conv_tail(c_ref[...], lg_ref[...], lb_ref[...], wo_ref[...])


def _conv_sample_body(x_ref, ext_ref, dw_ref, dwb_ref, lg_ref, lb_ref, wo_ref, o_ref, c_ref, *, taps):
    bb, ext_len, d = ext_ref.shape
    s = ext_len - (taps - 1)
    for b in range(bb):
        acc = jnp.zeros((s, d), F32) + dwb_ref[...]
        for k in range(taps):
            acc = acc + ext_ref[b, k:k + s, :] * dw_ref[k:k + 1, :]
        c_ref[b * s:(b + 1) * s, :] = acc
    o_ref[...] = x_ref[...] + _conv_tail(c_ref[...], lg_ref[...], lb_ref[...], wo_ref[...])


def _fox_qkv(x_ref, g_ref, wqkv_ref, wfh_ref, wfl_ref, bf_ref):
    d = x_ref.shape[1]
    hn32 = _rms(x_ref[...], g_ref[...])
    qkv = _dot(hn32.astype(BF16), wqkv_ref[...])
    lg = _dot_hi(hn32, wfh_ref[...], wfl_ref[...]) + bf_ref[...]
    return hn32, qkv[:, :d], qkv[:, d:2 * d], qkv[:, 2 * d:], _log_sigmoid(lg)


def _fox_proj_sample_body(x_ref, g_ref, wqkv_ref, wfh_ref, wfl_ref, bf_ref,
                          q_ref, k_ref, v_ref, lf_ref):
    _, q, k, v, logf = _fox_qkv(x_ref, g_ref, wqkv_ref, wfh_ref, wfl_ref, bf_ref)
    q_ref[...] = q.astype(BF16)
    k_ref[...] = k
    v_ref[...] = v
    lf_ref[...] = logf[:, :lf_ref.shape[1]]


def _fox_proj_prompt_body(x_ref, g_ref, wqkv_ref, wfh_ref, wfl_ref, bf_ref, wfth_ref, wftl_ref, bfc_ref,
                          hsel_ref, q_ref, k_ref, v_ref, kb_ref, vb_ref, lf_ref, f_ref, ft_ref, st_ref,
                          crow_ref, ccol_ref):
    tm = x_ref.shape[0]
    heads = lf_ref.shape[1]
    i = pl.program_id(0)

    @pl.when(i == 0)
    def _():
        crow_ref[...] = jnp.zeros_like(crow_ref)
        ccol_ref[...] = jnp.zeros_like(ccol_ref)

    hn32, q, k, v, logf = _fox_qkv(x_ref, g_ref, wqkv_ref, wfh_ref, wfl_ref, bf_ref)
    q_ref[...] = q.astype(BF16)
    k_ref[...] = k
    v_ref[...] = v
    kb_ref[...] = k.astype(BF16)
    vb_ref[...] = v.astype(BF16)
    lf_ref[...] = logf[:, :heads]

    lane = lax.broadcasted_iota(jnp.int32, (tm, LANES), 1)
    logf = jnp.where(lane < heads, logf, 0.0)
    hh, hl = _split2(hn32)
    lgt = _dot_nt(wfth_ref[...], hh) + _dot_nt(wfth_ref[...], hl) + _dot_nt(wftl_ref[...], hh) + bfc_ref[...]
    sub = lax.broadcasted_iota(jnp.int32, (LANES, tm), 0)
    logft = jnp.where(sub < heads, _log_sigmoid(lgt), 0.0)

    r = lax.broadcasted_iota(jnp.int32, (tm, tm), 0)
    c = lax.broadcasted_iota(jnp.int32, (tm, tm), 1)
    lower = jnp.where(c <= r, 1.0, 0.0).astype(BF16)
    upper = jnp.where(r <= c, 1.0, 0.0).astype(BF16)
    ones_r = jnp.ones((8, tm), BF16)
    ones_c = jnp.ones((tm, LANES), BF16)
    a1, a2, a3 = _split3(logf)
    b1, b2, b3 = _split3(logft)
    f_nat = _dot(lower, a1) + _dot(lower, a2) + _dot(lower, a3) + crow_ref[0:1, :]
    f_ref[...] = f_nat
    ft = _dot(b1, upper) + _dot(b2, upper) + _dot(b3, upper)
    ft_ref[...] = ft + jnp.tile(ccol_ref[...], (1, tm // LANES))
    crow_ref[...] = crow_ref[...] + _dot(ones_r, a1) + _dot(ones_r, a2) + _dot(ones_r, a3)
    ccol_ref[...] = ccol_ref[...] + _dot(b1, ones_c) + _dot(b2, ones_c) + _dot(b3, ones_c)

    def max_head_norm2(z):
        zh, zl = _split2(z * z)
        n2 = _dot(zh, hsel_ref[...]) + _dot(zl, hsel_ref[...])
        return jnp.max(jnp.max(n2, axis=1, keepdims=True), axis=0, keepdims=True)

    sub8 = lax.broadcasted_iota(jnp.int32, (8, LANES), 0)
    st_ref[0] = jnp.where(sub8 == 0, max_head_norm2(q),
                          jnp.where(sub8 == 1, max_head_norm2(k),
                                    jnp.where(sub8 == 2, f_nat[0:1, :],
                                              jnp.where(sub8 == 3, f_nat[tm - 1:tm, :], 0.0))))


def _fox_proj_prompt(x_all, weights, *, t_p, tm, heads):
    d = x_all.shape[1]
    tile = lambda cols: pl.BlockSpec((tm, cols), lambda i: (i, 0))
    sds = lambda cols, dt: jax.ShapeDtypeStruct((t_p, cols), dt)
    n = t_p // tm
    return pl.pallas_call(
        _fox_proj_prompt_body, grid=(n,),
        in_specs=[tile(d)] + [_full_spec(w) for w in weights],
        out_specs=[tile(d)] * 5 + [tile(heads), tile(LANES), pl.BlockSpec((LANES, tm), lambda i: (0, i)),
                                   pl.BlockSpec((1, 8, LANES), lambda i: (i, 0, 0))],
        out_shape=[sds(d, BF16), sds(d, F32), sds(d, F32), sds(d, BF16), sds(d, BF16), sds(heads, F32),
                   sds(LANES, F32), jax.ShapeDtypeStruct((LANES, t_p), F32),
                   jax.ShapeDtypeStruct((n, 8, LANES), F32)],
        scratch_shapes=[pltpu.VMEM((8, LANES), F32), pltpu.VMEM((LANES, LANES), F32)],
        compiler_params=_params(1), name="fox_proj_prompt")(x_all, *weights)


def _fox_attn_prompt_body(start_ref, x_ref, q_ref, k_hbm, v_hbm, f_ref, ft_ref, wo_ref, o_ref,
                          kbuf_ref, vbuf_ref, sem, acc_ref, m_ref, ob_ref, *, heads, hd):
    tq = q_ref.shape[0]
    tk = kbuf_ref.shape[1]
    i = pl.program_id(0)
    j0 = start_ref[i]
    n = i - j0 + 1

    def copies(j, slot):
        rows = pl.ds(pl.multiple_of(j * tk, tk), tk)
        return (pltpu.make_async_copy(k_hbm.at[rows], kbuf_ref.at[slot], sem.at[0, slot]),
                pltpu.make_async_copy(v_hbm.at[rows], vbuf_ref.at[slot], sem.at[1, slot]))

    for cp in copies(j0, 0):
        cp.start()
    acc_ref[...] = jnp.zeros_like(acc_ref)
    m_ref[...] = jnp.full_like(m_ref, NEG)

    lane_k = lax.broadcasted_iota(jnp.int32, (tk, LANES), 1)
    keep_lo = jnp.where(lane_k < hd, 1.0, 0.0).astype(BF16)
    keep_hi = jnp.where(lane_k >= hd, 1.0, 0.0).astype(BF16)
    ones_lo = jnp.where(lane_k == hd, 1.0, 0.0).astype(BF16)
    ones_hi = jnp.where(lane_k == 0, 1.0, 0.0).astype(BF16)

    def block(jj, diagonal):
        j = j0 + jj
        slot = lax.rem(jj, 2)

        @pl.when(jj + 1 < n)
        def _():
            for cp in copies(j + 1, 1 - slot):
                cp.start()

        for cp in copies(j, slot):
            cp.wait()
        if diagonal:
            causal = (lax.broadcasted_iota(jnp.int32, (tq, tk), 1)
                      <= lax.broadcasted_iota(jnp.int32, (tq, tk), 0))
        def scores(h):
            sl = slice((h // 2) * LANES, (h // 2 + 1) * LANES)
            kk = kbuf_ref[slot, :, sl] * (keep_lo if h % 2 == 0 else keep_hi)
            fk = (ft_ref[j, h:h + 1, :] - f_ref[0:1, h:h + 1]) * LOG2E
            t = _dot_nt(q_ref[:, sl], kk) - fk
            return jnp.where(causal, t, NEG) if diagonal else t

        t_next = scores(0)
        for h in range(heads):
            t = t_next
            if h + 1 < heads:
                t_next = scores(h + 1)
            sl = slice((h // 2) * LANES, (h // 2 + 1) * LANES)
            vp = vbuf_ref[slot, :, sl]
            vv = vp * keep_lo + ones_lo if h % 2 == 0 else vp * keep_hi + ones_hi
            m_prev = m_ref[h]
            m_new = jnp.maximum(m_prev, jnp.max(t, axis=1, keepdims=True))
            alpha = jnp.exp2(m_prev - m_new)
            pexp = jnp.exp2(t - jnp.tile(m_new, (1, tk // LANES))).astype(BF16)
            acc_ref[h] = alpha * acc_ref[h] + _dot(pexp, vv)
            m_ref[h] = m_new

    def off_diagonal(jj, carry):
        block(jj, False)
        return carry

    lax.fori_loop(0, n - 1, off_diagonal, 0)
    block(n - 1, True)

    lane_q = lax.broadcasted_iota(jnp.int32, (tq, LANES), 1)
    for p in range(heads // 2):
        ae = acc_ref[2 * p]
        ao = acc_ref[2 * p + 1]
        o = jnp.where(lane_q < hd, ae / ae[:, hd:hd + 1], ao / ao[:, 0:1])
        ob_ref[:, p * LANES:(p + 1) * LANES] = o.astype(BF16)
    o_ref[...] = x_ref[...] + _dot(ob_ref[...], wo_ref[...])


SKIP_MARGIN = 100.0


def _fox_band_starts(stats, heads):
    nq = stats.shape[0]
    qmax = jnp.sqrt(stats[:, 0, 0]) * (1.01 / LOG2E)
    kmax = jnp.sqrt(stats[:, 1, 0]) * 1.01
    f_first = stats[:, 2, :heads]
    f_last = stats[:, 3, :heads]
    gap = jnp.max(f_first[:, None, :] - f_last[None, :, :], axis=2)
    bound = gap + qmax[:, None] * (kmax[None, :] + kmax[:, None]) + 0.01
    tile = jnp.arange(nq, dtype=jnp.int32)
    needed = jnp.logical_or(bound >= -SKIP_MARGIN, tile[None, :] == tile[:, None])
    needed = jnp.logical_and(needed, tile[None, :] <= tile[:, None])
    return jnp.argmax(needed, axis=1).astype(jnp.int32)


def _fox_attn_prompt(x_all, q, kb, vb, f_nat, f_t, stats, w_out, *, t_p, tq, heads):
    d = x_all.shape[1]
    hd = d // heads
    nq = t_p // tq
    starts = _fox_band_starts(stats, heads)
    ft3 = jnp.swapaxes(f_t[:heads].reshape(heads, nq, tq), 0, 1)
    grid_spec = pltpu.PrefetchScalarGridSpec(
        num_scalar_prefetch=1, grid=(nq,),
        in_specs=[
            pl.BlockSpec((tq, d), lambda i, st: (i, 0)),
            pl.BlockSpec((tq, d), lambda i, st: (i, 0)),
            pl.BlockSpec(memory_space=pl.ANY),
            pl.BlockSpec(memory_space=pl.ANY),
            pl.BlockSpec((tq, LANES), lambda i, st: (i, 0)),
            pl.BlockSpec((nq, heads, tq), lambda i, st: (0, 0, 0)),
            pl.BlockSpec((d, d), lambda i, st: (0, 0)),
        ],
        out_specs=pl.BlockSpec((tq, d), lambda i, st: (i, 0)),
        scratch_shapes=[pltpu.VMEM((2, tq, d), BF16), pltpu.VMEM((2, tq, d), BF16),
                        pltpu.SemaphoreType.DMA((2, 2)),
                        pltpu.VMEM((heads, tq, LANES), F32), pltpu.VMEM((heads, tq, LANES), F32),
                        pltpu.VMEM((tq, d), BF16)])
    return pl.pallas_call(
        functools.partial(_fox_attn_prompt_body, heads=heads, hd=hd),
        grid_spec=grid_spec, out_shape=jax.ShapeDtypeStruct(x_all.shape, F32),
        input_output_aliases={1: 0}, compiler_params=_params(1), name="fox_attn_prompt",
    )(starts, x_all, q, kb, vb, f_nat, ft3, w_out)


def _fox_attn_sample_body(q_ref, kn_ref, vn_ref, ck_ref, cv_ref, lft_ref, o_ref,
                          qbd_ref, bias_ref, tail_ref, m_ref, l_ref, acc_ref, *, heads, n_chunks):
    s_new, d = q_ref.shape
    hd = d // heads
    rows = heads * s_new
    kc_len = ck_ref.shape[1]
    past = kc_len * n_chunks
    width = lft_ref.shape[2]
    kc = pl.program_id(1)
    row_h = lax.broadcasted_iota(jnp.int32, (rows, d), 0) // s_new
    head_mask = row_h == lax.broadcasted_iota(jnp.int32, (rows, d), 1) // hd

    @pl.when(kc == 0)
    def _():
        qt = jnp.concatenate([q_ref[...].astype(F32)] * heads, axis=0)
        qbd_ref[...] = jnp.where(head_mask, qt, 0.0).astype(BF16)
        lf = lft_ref[0]
        lane = lax.broadcasted_iota(jnp.int32, (heads, width), 1)
        run = lf
        shift = 1
        while shift < width:
            moved = pltpu.roll(run, width - shift, 1)
            run = run + jnp.where(lane + shift < width, moved, 0.0)
            shift *= 2
        rexcl = run - lf
        rexp = jnp.concatenate(
            [jnp.broadcast_to(rexcl[h:h + 1, :], (s_new, width)) for h in range(heads)], axis=0)
        tail = rexp[:, past:]
        tl = lax.broadcasted_iota(jnp.int32, tail.shape, 1)
        tr = lax.broadcasted_iota(jnp.int32, tail.shape, 0) % s_new
        rq = jnp.sum(jnp.where(tl == tr, tail, 0.0), axis=1, keepdims=True)
        for c in range(n_chunks):
            bias_ref[c] = rexp[:, c * kc_len:(c + 1) * kc_len] - rq
        tail_ref[...] = tail - rq
        m_ref[...] = jnp.full_like(m_ref, NEG)
        l_ref[...] = jnp.zeros_like(l_ref)
        acc_ref[...] = jnp.zeros_like(acc_ref)

    def update(s, vmat):
        m_prev = m_ref[...]
        m_new = jnp.maximum(m_prev, jnp.max(s, axis=1, keepdims=True))
        alpha = jnp.exp(m_prev - m_new)
        p = jnp.exp(s - m_new)
        l_ref[...] = alpha * l_ref[...] + jnp.sum(p, axis=1, keepdims=True)
        acc_ref[...] = alpha * acc_ref[...] + _dot(p.astype(BF16), vmat)
        m_ref[...] = m_new

    update(_dot_nt(qbd_ref[...], ck_ref[0]) + bias_ref[kc], cv_ref[0])

    @pl.when(kc == n_chunks - 1)
    def _():
        pad = jnp.zeros((width - past - s_new, d), BF16)
        kn = jnp.concatenate([kn_ref[...].astype(BF16), pad], axis=0)
        vn = jnp.concatenate([vn_ref[...].astype(BF16), pad], axis=0)
        s = _dot_nt(qbd_ref[...], kn) + tail_ref[...]
        tl = lax.broadcasted_iota(jnp.int32, s.shape, 1)
        tr = lax.broadcasted_iota(jnp.int32, s.shape, 0) % s_new
        update(jnp.where(tl <= tr, s, NEG), vn)
        o_full = jnp.where(head_mask, acc_ref[...] / l_ref[...], 0.0)
        out = o_full[0:s_new, :]
        for h in range(1, heads):
            out = out + o_full[h * s_new:(h + 1) * s_new, :]
        o_ref[...] = out.astype(BF16)


def _fox_attn_sample(q_s, k_s, v_s, cache_k, cache_v, lft_all, *, heads, s_new):
    t_s, d = q_s.shape
    n_b = t_s // s_new
    past = cache_k.shape[1]
    n_chunks = 1
    kc_len = past // n_chunks
    width = lft_all.shape[2]
    rows = heads * s_new
    return pl.pallas_call(
        functools.partial(_fox_attn_sample_body, heads=heads, n_chunks=n_chunks),
        grid=(n_b, n_chunks),
        in_specs=[
            pl.BlockSpec((s_new, d), lambda b, c: (b, 0)),
            pl.BlockSpec((s_new, d), lambda b, c: (b, 0)),
            pl.BlockSpec((s_new, d), lambda b, c: (b, 0)),
            pl.BlockSpec((1, kc_len, d), lambda b, c: (b, c, 0)),
            pl.BlockSpec((1, kc_len, d), lambda b, c: (b, c, 0)),
            pl.BlockSpec((1, heads, width), lambda b, c: (b, 0, 0)),
        ],
        out_specs=pl.BlockSpec((s_new, d), lambda b, c: (b, 0)),
        out_shape=jax.ShapeDtypeStruct((t_s, d), BF16),
        scratch_shapes=[pltpu.VMEM((rows, d), BF16), pltpu.VMEM((n_chunks, rows, kc_len), F32),
                        pltpu.VMEM((rows, width - past), F32),
                        pltpu.VMEM((rows, 1), F32), pltpu.VMEM((rows, 1), F32),
                        pltpu.VMEM((rows, d), F32)],
        compiler_params=_params(2), name="fox_attn_sample")(q_s, k_s, v_s, cache_k, cache_v, lft_all)


def _proj_residual_body(x_ref, o_ref_in, w_ref, o_ref):
    o_ref[...] = x_ref[...] + _dot(o_ref_in[...], w_ref[...])


def _sgu_body(x_ref, g_ref, win_ref, lg_ref, lb_ref, ws_ref, bs_ref, wo_ref, o_ref, v_ref, mix_ref,
              *, n_prompt_tiles, groups):
    tm, d = x_ref.shape
    gc = d // groups
    i = pl.program_id(0)
    sel = (i >= n_prompt_tiles).astype(jnp.int32)
    hn = _rms(x_ref[...], g_ref[...]).astype(BF16)
    z = _dot(hn, win_ref[...])
    z = 0.5 * z * (1.0 + lax.erf(z * (2.0 ** -0.5)))
    u = z[:, :d]
    v = _ln(z[:, d:], lg_ref[...], lb_ref[...])
    v_ref[...] = v
    vb = v.astype(BF16)
    for c in range(tm // SGU_CHUNK):
        rs = slice(c * SGU_CHUNK, (c + 1) * SGU_CHUNK)
        for g in range(groups):
            cs = slice(g * gc, (g + 1) * gc)
            mix_ref[rs, cs] = _dot(ws_ref[sel, g], vb[rs, cs])
        mix_ref[rs, :] = mix_ref[rs, :] + bs_ref[sel]
    o_ref[...] = x_ref[...] + _dot((u * mix_ref[...]).astype(BF16), wo_ref[...])


def _xattn_core(q, k_of, v_of, heads):
    d = q.shape[1]
    hd = d // heads
    outs = []
    for h in range(heads):
        s = _dot_nt(q[:, h * hd:(h + 1) * hd], k_of(h))
        p = jnp.exp(s - jnp.max(s, axis=1, keepdims=True))
        p = p / jnp.sum(p, axis=1, keepdims=True)
        outs.append(_dot(p.astype(BF16), v_of(h)).astype(BF16))
    return outs


def _xattn_prompt_body(x_ref, g_ref, wq_ref, k_ref, v_ref, wo_ref, o_ref, *, heads, layer):
    d = x_ref.shape[1]
    hd = d // heads
    x = x_ref[...]
    q = _dot(_rms(x, g_ref[...]).astype(BF16), wq_ref[...]).astype(BF16)
    k = k_ref[layer].astype(BF16)
    v = v_ref[layer].astype(BF16)
    outs = _xattn_core(q, lambda h: k[:, h * hd:(h + 1) * hd], lambda h: v[:, h * hd:(h + 1) * hd], heads)
    y = x
    for h, oh in enumerate(outs):
        y = y + _dot(oh, wo_ref[h * hd:(h + 1) * hd, :])
    o_ref[...] = y


def _xattn_sample_body(x_ref, g_ref, wq_ref, k_ref, v_ref, wo_ref, o_ref, ob_ref, *, heads, s_new):
    d = x_ref.shape[1]
    hd = d // heads
    bb = k_ref.shape[1]
    x = x_ref[...]
    q = _dot(_rms(x, g_ref[...]).astype(BF16), wq_ref[...]).astype(BF16)
    for b in range(bb):
        rs = slice(b * s_new, (b + 1) * s_new)
        outs = _xattn_core(q[rs, :], lambda h, b=b: k_ref[0, b, :, h, :].astype(BF16),
                           lambda h, b=b: v_ref[0, b, :, h, :].astype(BF16), heads)
        for h, oh in enumerate(outs):
            ob_ref[rs, h * hd:(h + 1) * hd] = oh
    o_ref[...] = x + _dot(ob_ref[...], wo_ref[...])


def _ffn_body(x_ref, g_ref, wg_ref, wu_ref, wd_ref, o_ref, hn_ref, acc_ref):
    f = pl.program_id(1)

    @pl.when(f == 0)
    def _():
        hn_ref[...] = _rms(x_ref[...], g_ref[...]).astype(BF16)
        acc_ref[...] = jnp.zeros_like(acc_ref)

    hn = hn_ref[...]
    a = _silu(_dot(hn, wg_ref[...])) * _dot(hn, wu_ref[...])
    acc_ref[...] += _dot(a.astype(BF16), wd_ref[...])

    @pl.when(f == pl.num_programs(1) - 1)
    def _():
        o_ref[...] = x_ref[...] + acc_ref[...]


def _ffn(x_all, g, w_gu, w_down, *, tm, tf):
    m, d = x_all.shape
    ff = w_down.shape[0]
    nf = ff // tf
    return pl.pallas_call(
        _ffn_body, grid=(m // tm, nf),
        in_specs=[pl.BlockSpec((tm, d), lambda i, f: (i, 0)),
                  _full_spec(g),
                  pl.BlockSpec((d, tf), lambda i, f: (0, f)),
                  pl.BlockSpec((d, tf), lambda i, f, nf=nf: (0, nf + f)),
                  pl.BlockSpec((tf, d), lambda i, f: (f, 0))],
        out_specs=pl.BlockSpec((tm, d), lambda i, f: (i, 0)),
        out_shape=jax.ShapeDtypeStruct((m, d), F32),
        input_output_aliases={0: 0},
        scratch_shapes=[pltpu.VMEM((tm, d), BF16), pltpu.VMEM((tm, d), F32)],
        compiler_params=_params(2), name="ffn")(x_all, g, w_gu, w_gu, w_down)


def _router_body(x_ref, g_ref, rh_ref, rl_ref, idx_ref, gate_ref, rank_ref, cnt_ref, carry_ref, *, n_experts):
    tm = x_ref.shape[0]

    @pl.when(pl.program_id(0) == 0)
    def _():
        carry_ref[...] = jnp.zeros_like(carry_ref)

    hn32 = _rms(x_ref[...], g_ref[...])
    logits = _dot_hi(hn32, rh_ref[...], rl_ref[...])
    lane = lax.broadcasted_iota(jnp.int32, logits.shape, 1)
    lane_f = lane.astype(F32)
    big = float(LANES)
    masked = jnp.where(lane < n_experts, logits, -jnp.inf)
    m1 = jnp.max(masked, axis=1, keepdims=True)
    i1 = jnp.min(jnp.where(masked == m1, lane_f, big), axis=1, keepdims=True)
    masked2 = jnp.where(lane_f == i1, -jnp.inf, masked)
    m2 = jnp.max(masked2, axis=1, keepdims=True)
    i2 = jnp.min(jnp.where(masked2 == m2, lane_f, big), axis=1, keepdims=True)
    e = jnp.exp(m2 - m1)
    w1 = 1.0 / (1.0 + e)
    w2 = e * w1
    idx_ref[...] = jnp.where(lane == 0, i1, jnp.where(lane == 1, i2, 0.0)).astype(jnp.int32)
    gate_ref[...] = jnp.where(lane == 0, w1, jnp.where(lane == 1, w2, 0.0))
    hit1 = lane_f == i1
    hit2 = lane_f == i2
    onehot = jnp.where(jnp.logical_or(hit1, hit2), 1.0, 0.0).astype(BF16)
    r = lax.broadcasted_iota(jnp.int32, (tm, tm), 0)
    c = lax.broadcasted_iota(jnp.int32, (tm, tm), 1)
    before = jnp.where(c < r, 1.0, 0.0).astype(BF16)
    base = _dot(before, onehot) + carry_ref[0:1, :]
    r1 = jnp.sum(jnp.where(hit1, base, 0.0), axis=1, keepdims=True)
    r2 = jnp.sum(jnp.where(hit2, base, 0.0), axis=1, keepdims=True)
    rank_ref[...] = jnp.where(lane == 0, r1, jnp.where(lane == 1, r2, 0.0)).astype(jnp.int32)
    carry_ref[...] = carry_ref[...] + _dot(jnp.ones((8, tm), BF16), onehot)
    cnt_ref[...] = carry_ref[...]


def _moe_router(x_all, g, r_hi, r_lo, *, tm, n_experts):
    m, d = x_all.shape
    tile = pl.BlockSpec((tm, LANES), lambda i: (i, 0))
    return pl.pallas_call(
        functools.partial(_router_body, n_experts=n_experts), grid=(m // tm,),
        in_specs=[pl.BlockSpec((tm, d), lambda i: (i, 0)), _full_spec(g), _full_spec(r_hi), _full_spec(r_lo)],
        out_specs=[tile, tile, tile, pl.BlockSpec((8, LANES), lambda i: (0, 0))],
        out_shape=[jax.ShapeDtypeStruct((m, LANES), jnp.int32), jax.ShapeDtypeStruct((m, LANES), F32),
                   jax.ShapeDtypeStruct((m, LANES), jnp.int32), jax.ShapeDtypeStruct((8, LANES), F32)],
        scratch_shapes=[pltpu.VMEM((8, LANES), F32)],
        compiler_params=_params(1), name="moe_router")(x_all, g, r_hi, r_lo)


def _row_copies(src_ref, dst_ref, idx_ref, base, sem, n, *, gather, stride=1, offset=0):
    def copy(j):
        row = idx_ref[base + j * stride + offset]
        if gather:
            return pltpu.make_async_copy(src_ref.at[pl.ds(row, 1)], dst_ref.at[pl.ds(j, 1)], sem)
        return pltpu.make_async_copy(src_ref.at[pl.ds(j, 1)], dst_ref.at[pl.ds(row, 1)], sem)

    def wait(j, c):
        copy(j).wait()
        return c

    for j in range(n):
        copy(j).start()
    lax.fori_loop(0, n, wait, 0, unroll=8)


def _moe_dispatch_body(dest_ref, x_ref, g_ref, init_hbm, out_hbm, hbuf_ref, sem0, sem1):
    del init_hbm
    tm = x_ref.shape[0]
    base = pl.program_id(0) * (tm * TOP_K)
    hbuf_ref[...] = _rms(x_ref[...], g_ref[...])
    _row_copies(hbuf_ref, out_hbm, dest_ref, base, sem0, tm, gather=False, stride=TOP_K, offset=0)
    _row_copies(hbuf_ref, out_hbm, dest_ref, base, sem1, tm, gather=False, stride=TOP_K, offset=1)


def _moe_dispatch(x_all, g, dest, *, tm, rows_sorted):
    m, d = x_all.shape
    grid_spec = pltpu.PrefetchScalarGridSpec(
        num_scalar_prefetch=1, grid=(m // tm,),
        in_specs=[pl.BlockSpec((tm, d), lambda i, dest: (i, 0)),
                  pl.BlockSpec((1, d), lambda i, dest: (0, 0)),
                  pl.BlockSpec(memory_space=pl.ANY)],
        out_specs=pl.BlockSpec(memory_space=pl.ANY),
        scratch_shapes=[pltpu.VMEM((tm, d), F32), pltpu.SemaphoreType.DMA(()), pltpu.SemaphoreType.DMA(())])
    return pl.pallas_call(
        _moe_dispatch_body, grid_spec=grid_spec, out_shape=jax.ShapeDtypeStruct((rows_sorted, d), F32),
        input_output_aliases={3: 0}, compiler_params=_params(1), name="moe_dispatch",
    )(dest, x_all, g, jnp.zeros((rows_sorted, d), F32))


def _moe_ffn_body(te_ref, nt_ref, hs_ref, wg_ref, wu_ref, wd_ref, o_ref, hb_ref, acc_ref):
    r = pl.program_id(0)
    f = pl.program_id(1)
    valid = r < nt_ref[0]

    @pl.when(jnp.logical_and(valid, f == 0))
    def _():
        hb_ref[...] = hs_ref[...].astype(BF16)
        acc_ref[...] = jnp.zeros_like(acc_ref)

    @pl.when(valid)
    def _():
        h = hb_ref[...]
        a = _silu(_dot(h, wg_ref[0, 0])) * _dot(h, wu_ref[0, 0])
        acc_ref[...] += _dot(a.astype(BF16), wd_ref[0, 0])

    last = f == pl.num_programs(1) - 1

    @pl.when(jnp.logical_and(valid, last))
    def _():
        o_ref[...] = acc_ref[...]

    @pl.when(jnp.logical_and(jnp.logical_not(valid), last))
    def _():
        o_ref[...] = jnp.zeros_like(o_ref)


def _moe_ffn(hn_sorted, tile_expert, n_tiles, w_gu, w_down, *, layer, tm, tf):
    rows, d = hn_sorted.shape
    d_exp = w_down.shape[2]
    nf = d_exp // tf

    def fe(r, f, nt):
        return jnp.where(r < nt[0], f, nf - 1)

    grid_spec = pltpu.PrefetchScalarGridSpec(
        num_scalar_prefetch=2, grid=(rows // tm, nf),
        in_specs=[
            pl.BlockSpec((tm, d), lambda r, f, te, nt: (r, 0)),
            pl.BlockSpec((1, 1, d, tf), lambda r, f, te, nt: (layer, te[r], 0, fe(r, f, nt))),
            pl.BlockSpec((1, 1, d, tf), lambda r, f, te, nt: (layer, te[r], 0, nf + fe(r, f, nt))),
            pl.BlockSpec((1, 1, tf, d), lambda r, f, te, nt: (layer, te[r], fe(r, f, nt), 0)),
        ],
        out_specs=pl.BlockSpec((tm, d), lambda r, f, te, nt: (r, 0)),
        scratch_shapes=[pltpu.VMEM((tm, d), BF16), pltpu.VMEM((tm, d), F32)])
    return pl.pallas_call(
        _moe_ffn_body, grid_spec=grid_spec, out_shape=jax.ShapeDtypeStruct((rows, d), F32),
        compiler_params=_params(2), name="moe_ffn")(tile_expert, n_tiles, hn_sorted, w_gu, w_gu, w_down)


def _moe_combine_nonorm_body(dest_ref, x_ref, gate_ref, y_hbm, o_ref, b0_ref, b1_ref, sem0, sem1):
    _moe_combine_body(dest_ref, x_ref, gate_ref, y_hbm, None, o_ref, b0_ref, b1_ref, sem0, sem1)


def _moe_combine_body(dest_ref, x_ref, gate_ref, y_hbm, gn_ref, o_ref, b0_ref, b1_ref, sem0, sem1):
    tm = x_ref.shape[0]
    base = pl.program_id(0) * (tm * TOP_K)
    _row_copies(y_hbm, b0_ref, dest_ref, base, sem0, tm, gather=True, stride=TOP_K, offset=0)
    _row_copies(y_hbm, b1_ref, dest_ref, base, sem1, tm, gather=True, stride=TOP_K, offset=1)
    gate = gate_ref[...]
    y = x_ref[...] + gate[:, 0:1] * b0_ref[...] + gate[:, 1:2] * b1_ref[...]
    o_ref[...] = y if gn_ref is None else _rms(y, gn_ref[...])


def _moe_combine(x_all, gates, dest, y_sorted, *, tm, final_gain=None):
    m, d = x_all.shape
    body = _moe_combine_body
    extra_specs, extra_args = [], []
    if final_gain is None:
        body = _moe_combine_nonorm_body
    else:
        extra_specs, extra_args = [pl.BlockSpec((1, d), lambda i, dest: (0, 0))], [final_gain]
    grid_spec = pltpu.PrefetchScalarGridSpec(
        num_scalar_prefetch=1, grid=(m // tm,),
        in_specs=[pl.BlockSpec((tm, d), lambda i, dest: (i, 0)),
                  pl.BlockSpec((tm, LANES), lambda i, dest: (i, 0)),
                  pl.BlockSpec(memory_space=pl.ANY)] + extra_specs,
        out_specs=pl.BlockSpec((tm, d), lambda i, dest: (i, 0)),
        scratch_shapes=[pltpu.VMEM((tm, d), F32), pltpu.VMEM((tm, d), F32),
                        pltpu.SemaphoreType.DMA(()), pltpu.SemaphoreType.DMA(())])
    return pl.pallas_call(
        body, grid_spec=grid_spec, out_shape=jax.ShapeDtypeStruct((m, d), F32),
        input_output_aliases={1: 0}, compiler_params=_params(1), name="moe_combine",
    )(dest, x_all, gates, y_sorted, *extra_args)


def _moe_plan(top_i, rank, counts, tm, r_max):
    n_experts = counts.shape[0]
    tiles = (counts + tm - 1) // tm
    tile_end = jnp.cumsum(tiles)
    row_start = (tile_end - tiles) * tm
    onehot = top_i[:, :, None] == jnp.arange(n_experts, dtype=jnp.int32)[None, None, :]
    dest = (jnp.sum(jnp.where(onehot, row_start[None, None, :], 0), axis=2) + rank).astype(jnp.int32)
    n_tiles = tile_end[-1:].astype(jnp.int32)
    tile_id = jnp.arange(r_max, dtype=jnp.int32)
    tile_expert = jnp.minimum(jnp.sum((tile_id[:, None] >= tile_end[None, :]).astype(jnp.int32), axis=1),
                              n_experts - 1).astype(jnp.int32)
    return dest.reshape(-1), n_tiles, tile_expert


def _final_norm_body(x_ref, g_ref, o_ref):
    o_ref[...] = _rms(x_ref[...], g_ref[...])


def _pick(n, cands):
    for c in cands:
        if n % c == 0:
            return c
    raise ValueError(f"no tile in {cands} divides {n}")


def kernel(x_prompt, x_sample, mem_prompt, cache_mem_k, cache_mem_v, state_conv, cache_fox_k, cache_fox_v, cache_fox_logf, norm_mix, norm_mem_q, norm_ffn, norm_mem, norm_final, conv_w_in, conv_dw, conv_dw_b, conv_ln_g, conv_ln_b, conv_w_out, fox_w_in, fox_b_f, fox_w_out, sgu_w_in, sgu_ln_g, sgu_ln_b, sgu_w_s, sgu_b_s, sgu_w_out, xa_w_q, xa_w_kv, xa_w_o, ffn_w_gu, ffn_w_down, moe_router, moe_w_gu, moe_w_down):
    batch, t_p, d = x_prompt.shape
    n_b, s_new, _ = x_sample.shape
    assert batch == 1
    t_s = n_b * s_new
    m = t_p + t_s
    depth = norm_mix.shape[0]
    n_mem = mem_prompt.shape[1]
    xa_heads = cache_mem_k.shape[3]
    taps = conv_dw.shape[1]
    fox_heads = fox_b_f.shape[1]
    past = cache_fox_k.shape[2]
    groups = sgu_w_s.shape[1]
    n_experts = moe_router.shape[2]
    d_exp = moe_w_down.shape[2]
    d_ff = ffn_w_down.shape[1]

    tm = _pick(int(np.gcd(t_p, t_s)), (512, 256, 128))
    n_pt = t_p // tm
    n_st = t_s // tm
    n_t = n_pt + n_st
    halo = 32
    assert taps - 1 <= halo and tm % halo == 0 and tm % SGU_CHUNK == 0 and SGU_CHUNK % s_new == 0
    assert fox_heads % 2 == 0 and d // fox_heads * 2 == LANES and fox_heads <= LANES
    assert past % SGU_CHUNK == 0 and s_new <= SGU_BLOCK

    row = lambda a: a.reshape(1, -1)
    x_all = jnp.concatenate([x_prompt[0], x_sample.reshape(t_s, d)], axis=0)

    mem_k, mem_v = _mem_kv(mem_prompt[0], row(norm_mem), xa_w_kv.astype(BF16))
    moe_w_gu_b = moe_w_gu.astype(BF16)
    moe_w_down_b = moe_w_down.astype(BF16)

    conv_p, conv_s = [], []
    fox_out = []
    sgu_v = []
    for i in range(depth):
        kind = i % 3
        j = i // 3
        g_mix = row(norm_mix[i])
        if kind == 0:
            (u_all,) = _row_call(
                _conv_in_body, grid_tiles=n_t, tm=tm, first_tile=0, row_ins=[x_all],
                full_ins=[g_mix, conv_w_in[j].astype(BF16)], outs=[(m, d, F32, 0)], alias_x=False,
                name="conv_in")
            dw = jnp.concatenate([conv_dw[j], jnp.zeros((halo - taps, d), F32)], axis=0)
            tail_w = [dw, row(conv_dw_b[j]), row(conv_ln_g[j]), row(conv_ln_b[j]), conv_w_out[j].astype(BF16)]
            hb = tm // halo
            prompt_w = [jnp.broadcast_to(conv_dw[j][:, None, :], (taps, 8, d))] + tail_w[1:]
            x_all = pl.pallas_call(
                functools.partial(_conv_prompt_body, taps=taps, halo=halo, rc=32), grid=(n_pt,),
                in_specs=[pl.BlockSpec((tm, d), lambda t: (t, 0)),
                          pl.BlockSpec((tm, d), lambda t: (t, 0)),
                          pl.BlockSpec((halo, d), lambda t, hb=hb: (jnp.maximum(t * hb - 1, 0), 0))]
                         + [_full_spec(a) for a in prompt_w],
                out_specs=pl.BlockSpec((tm, d), lambda t: (t, 0)),
                out_shape=jax.ShapeDtypeStruct((m, d), F32), input_output_aliases={0: 0},
                scratch_shapes=[pltpu.VMEM((tm + halo + 8, d), F32),
                                pltpu.VMEM((8, tm + 8 * ((taps - 1) // 8), d), F32),
                                pltpu.VMEM((tm, d), F32)],
                compiler_params=_params(1), name="conv_prompt")(x_all, u_all, u_all, *prompt_w)
            u_s = u_all[t_p:].reshape(n_b, s_new, d)
            ext_s = jnp.concatenate([state_conv[j], u_s], axis=1)
            bb = _pick(n_b, (8, 4, 2, 1))
            rows_s = bb * s_new
            x_all = pl.pallas_call(
                functools.partial(_conv_sample_body, taps=taps), grid=(n_b // bb,),
                in_specs=[pl.BlockSpec((rows_s, d), lambda t, o=t_p // rows_s: (t + o, 0)),
                          pl.BlockSpec((bb, taps - 1 + s_new, d), lambda t: (t, 0, 0))]
                         + [_full_spec(a) for a in tail_w],
                out_specs=pl.BlockSpec((rows_s, d), lambda t, o=t_p // rows_s: (t + o, 0)),
                out_shape=jax.ShapeDtypeStruct((m, d), F32), input_output_aliases={0: 0},
                scratch_shapes=[pltpu.VMEM((rows_s, d), F32)],
                compiler_params=_params(1), name="conv_sample")(x_all, ext_s, *tail_w)
            conv_p.append(u_all[t_p - (taps - 1):t_p][None])
            conv_s.append(ext_s[:, s_new:])
        elif kind == 1:
            hd = d // fox_heads
            w_in = fox_w_in[j]
            w_qkv = jnp.concatenate([w_in[:, :d] * (hd ** -0.5), w_in[:, d:3 * d]], axis=1).astype(BF16)
            w_qkv_p = jnp.concatenate([w_in[:, :d] * (hd ** -0.5 * LOG2E), w_in[:, d:3 * d]],
                                      axis=1).astype(BF16)
            w_f = jnp.concatenate([w_in[:, 3 * d:], jnp.zeros((d, LANES - fox_heads), F32)], axis=1)
            wf_hi = w_f.astype(BF16)
            wf_lo = (w_f - wf_hi.astype(F32)).astype(BF16)
            b_f = jnp.concatenate([fox_b_f[j], jnp.zeros((LANES - fox_heads,), F32)])
            b_row = row(b_f)
            ta = min(tm, FOX_TILE)
            b_col = jnp.broadcast_to(b_f[:, None], (LANES, ta))
            head_sel = (jnp.arange(d)[:, None] // hd == jnp.arange(LANES)[None, :]).astype(BF16)
            proj_w = [g_mix, w_qkv, wf_hi, wf_lo, b_row]
            q_p, k_p, v_p, kb_p, vb_p, lf_p, f_nat, f_t, stats = _fox_proj_prompt(
                x_all, [g_mix, w_qkv_p, wf_hi, wf_lo, b_row, wf_hi.T, wf_lo.T, b_col, head_sel],
                t_p=t_p, tm=ta, heads=fox_heads)
            w_out = fox_w_out[j].astype(BF16)
            x_all = _fox_attn_prompt(x_all, q_p, kb_p, vb_p, f_nat, f_t, stats, w_out, t_p=t_p, tq=ta,
                                     heads=fox_heads)
            q_s, k_s, v_s, lf_s = _row_call(
                _fox_proj_sample_body, grid_tiles=n_st, tm=tm, first_tile=n_pt, row_ins=[x_all],
                full_ins=proj_w,
                outs=[(t_s, d, BF16, 0), (t_s, d, F32, 0), (t_s, d, F32, 0), (t_s, fox_heads, F32, 0)],
                alias_x=False, name="fox_proj_sample")
            width = past + LANES
            lft_all = jnp.concatenate(
                [jnp.swapaxes(cache_fox_logf[j], 1, 2),
                 jnp.swapaxes(lf_s.reshape(n_b, s_new, fox_heads), 1, 2),
                 jnp.zeros((n_b, fox_heads, width - past - s_new), F32)], axis=2)
            o_s = _fox_attn_sample(q_s, k_s, v_s, cache_fox_k[j].astype(BF16).reshape(n_b, past, d),
                                   cache_fox_v[j].astype(BF16).reshape(n_b, past, d), lft_all,
                                   heads=fox_heads, s_new=s_new)
            x_all = pl.pallas_call(
                _proj_residual_body, grid=(n_st,),
                in_specs=[pl.BlockSpec((tm, d), lambda t, o=n_pt: (t + o, 0)),
                          pl.BlockSpec((tm, d), lambda t: (t, 0)), _full_spec(w_out)],
                out_specs=pl.BlockSpec((tm, d), lambda t, o=n_pt: (t + o, 0)),
                out_shape=jax.ShapeDtypeStruct((m, d), F32), input_output_aliases={0: 0},
                compiler_params=_params(1), name="fox_out_sample")(x_all, o_s, w_out)
            fox_out.append((k_p, v_p, lf_p, k_s, v_s, lf_s))
        else:
            ws = sgu_w_s[j]
            pos = jnp.arange(SGU_CHUNK)
            blk_mask = (pos[None, :] // SGU_BLOCK) <= (pos[:, None] // SGU_BLOCK)
            ws_p = jnp.where(blk_mask[None], ws, 0.0)
            reps = SGU_CHUNK // s_new
            eye = jnp.eye(reps, dtype=F32)
            ws_s = jnp.einsum("ab,gij->gaibj", eye, ws_p[:, :s_new, :s_new]).reshape(
                groups, SGU_CHUNK, SGU_CHUNK)
            ws_all = jnp.stack([ws_p, ws_s]).astype(BF16)
            gc = d // groups
            bs_p = jnp.repeat(sgu_b_s[j].T, gc, axis=1)
            bs_s = jnp.tile(bs_p[:s_new], (reps, 1))
            bs_all = jnp.stack([bs_p, bs_s])
            x_all, v_all = _row_call(
                functools.partial(_sgu_body, n_prompt_tiles=n_pt, groups=groups),
                grid_tiles=n_t, tm=tm, first_tile=0, row_ins=[x_all],
                full_ins=[g_mix, sgu_w_in[j].astype(BF16), row(sgu_ln_g[j]), row(sgu_ln_b[j]),
                          ws_all, bs_all, sgu_w_out[j].astype(BF16)],
                outs=[(m, d, F32, 0), (m, d, F32, 0)], scratch=[pltpu.VMEM((tm, d), F32)], name="sgu")
            sgu_v.append(v_all[t_p:].reshape(n_b, s_new, d))

        xa_hd = d // xa_heads
        wq = (xa_w_q[i] * (xa_hd ** -0.5)).astype(BF16)
        wo = xa_w_o[i].astype(BF16)
        g_q = row(norm_mem_q[i])
        (x_all,) = _row_call(
            functools.partial(_xattn_prompt_body, heads=xa_heads, layer=i),
            grid_tiles=n_pt, tm=tm, first_tile=0, row_ins=[x_all],
            full_ins=[g_q, wq, mem_k, mem_v, wo], outs=[(m, d, F32, 0)], name="xattn_prompt")
        bb = _pick(n_b, (4, 2, 1))
        rows_s = bb * s_new
        x_all = pl.pallas_call(
            functools.partial(_xattn_sample_body, heads=xa_heads, s_new=s_new), grid=(n_b // bb,),
            in_specs=[pl.BlockSpec((rows_s, d), lambda t, o=t_p // rows_s: (t + o, 0)),
                      _full_spec(g_q), _full_spec(wq),
                      pl.BlockSpec((1, bb, n_mem, xa_heads, xa_hd), lambda t, i=i: (i, t, 0, 0, 0)),
                      pl.BlockSpec((1, bb, n_mem, xa_heads, xa_hd), lambda t, i=i: (i, t, 0, 0, 0)),
                      _full_spec(wo)],
            out_specs=pl.BlockSpec((rows_s, d), lambda t, o=t_p // rows_s: (t + o, 0)),
            out_shape=jax.ShapeDtypeStruct((m, d), F32), input_output_aliases={0: 0},
            scratch_shapes=[pltpu.VMEM((rows_s, d), BF16)],
            compiler_params=_params(1), name="xattn_sample")(x_all, g_q, wq, cache_mem_k, cache_mem_v, wo)

        g_f = row(norm_ffn[i])
        c = i // 2
        if i % 2 == 0:
            tf = _pick(d_ff, (1408, 1024, 512, 256, 128))
            x_all = _ffn(x_all, g_f, ffn_w_gu[c].astype(BF16), ffn_w_down[c].astype(BF16), tm=tm, tf=tf)
        else:
            router = jnp.concatenate([moe_router[c], jnp.zeros((d, LANES - n_experts), F32)], axis=1)
            r_hi = router.astype(BF16)
            r_lo = (router - r_hi.astype(F32)).astype(BF16)
            idx, gates, rank, counts = _moe_router(x_all, g_f, r_hi, r_lo, tm=tm, n_experts=n_experts)
            r_max = (TOP_K * m + n_experts * (tm - 1)) // tm
            dest, n_tiles, tile_expert = _moe_plan(idx[:, :TOP_K], rank[:, :TOP_K],
                                                   counts[0, :n_experts].astype(jnp.int32), tm, r_max)
            hn_sorted = _moe_dispatch(x_all, g_f, dest, tm=tm, rows_sorted=r_max * tm)
            tf = _pick(d_exp, (1792, 1024, 512, 256, 128))
            y_sorted = _moe_ffn(hn_sorted, tile_expert, n_tiles, moe_w_gu_b, moe_w_down_b, layer=c, tm=tm, tf=tf)
            last = i == depth - 1
            x_all = _moe_combine(x_all, gates, dest, y_sorted, tm=tm,
                                 final_gain=row(norm_final) if last else None)

    if depth % 2 == 0:
        y_all = x_all
    else:
        (y_all,) = _row_call(_final_norm_body, grid_tiles=n_t, tm=tm, first_tile=0, row_ins=[x_all],
                             full_ins=[row(norm_final)], outs=[(m, d, F32, 0)], alias_x=False,
                             name="final_norm")
    y_prompt = y_all[:t_p][None]
    y_sample = y_all[t_p:].reshape(n_b, s_new, d)
    mem_shape = (depth, 1, n_mem, xa_heads, d // xa_heads)
    fk_p, fv_p, fl_p, fk_s, fv_s, fl_s = zip(*fox_out)
    hshape_p = (1, t_p, fox_heads, d // fox_heads)
    hshape_s = (n_b, s_new, fox_heads, d // fox_heads)
    return (y_prompt, y_sample, mem_k.reshape(mem_shape), mem_v.reshape(mem_shape),
            jnp.stack(conv_p), jnp.stack(conv_s),
            jnp.stack([a.reshape(hshape_p) for a in fk_p]), jnp.stack([a.reshape(hshape_p) for a in fv_p]),
            jnp.stack([a.reshape(1, t_p, fox_heads) for a in fl_p]),
            jnp.stack([a.reshape(hshape_s) for a in fk_s]), jnp.stack([a.reshape(hshape_s) for a in fv_s]),
            jnp.stack([a.reshape(n_b, s_new, fox_heads) for a in fl_s]),
            jnp.stack(sgu_v))
```

```python
import functools

import numpy as np
import jax
import jax.numpy as jnp
from jax import lax
from jax.experimental import pallas as pl
from jax.experimental.pallas import tpu as pltpu

F32 = jnp.float32
BF16 = jnp.bfloat16
RMS_EPS = 1e-6
LN_EPS = 1e-5
SGU_CHUNK = 128
SGU_BLOCK = 64
TOP_K = 2
LANES = 128
NEG = -1e30
LOG2E = 1.4426950408889634
VMEM_LIMIT = 52 * 1024 * 1024
FOX_TILE = 512


def _params(n_axes, vmem=VMEM_LIMIT):
    return pltpu.CompilerParams(dimension_semantics=("arbitrary",) * n_axes, vmem_limit_bytes=vmem)


def _dot(a, b):
    return jnp.dot(a, b, preferred_element_type=F32)


def _dot_nt(a, b):
    return lax.dot_general(a, b, (((1,), (1,)), ((), ())), preferred_element_type=F32)


def _rms(x, g):
    return x * lax.rsqrt(jnp.mean(x * x, axis=-1, keepdims=True) + RMS_EPS) * g


def _ln(x, g, b):
    mu = jnp.mean(x, axis=-1, keepdims=True)
    xc = x - mu
    var = jnp.mean(xc * xc, axis=-1, keepdims=True)
    return xc * lax.rsqrt(var + LN_EPS) * g + b


def _silu(x):
    return x * jax.nn.sigmoid(x)


def _log_sigmoid(x):
    return jnp.minimum(x, 0.0) - jnp.log(1.0 + jnp.exp(-jnp.abs(x)))


def _split3(x):
    a = x.astype(BF16)
    r = x - a.astype(F32)
    b = r.astype(BF16)
    c = (r - b.astype(F32)).astype(BF16)
    return a, b, c


def _split2(x):
    a = x.astype(BF16)
    return a, (x - a.astype(F32)).astype(BF16)


def _dot_hi(x32, w_hi, w_lo):
    xh, xl = _split2(x32)
    return _dot(xh, w_hi) + _dot(xl, w_hi) + _dot(xh, w_lo)


def _full_spec(a):
    nd = a.ndim
    return pl.BlockSpec(a.shape, lambda *_, nd=nd: (0,) * nd)


def _row_call(body, *, grid_tiles, tm, first_tile, row_ins, full_ins, outs, alias_x=True,
              scratch=(), name=None):
    in_specs = [pl.BlockSpec((tm, a.shape[1]), lambda i, ft=first_tile: (i + ft, 0)) for a in row_ins]
    in_specs += [_full_spec(a) for a in full_ins]
    out_shape, out_specs = [], []
    for rows, cols, dt, off in outs:
        out_shape.append(jax.ShapeDtypeStruct((rows, cols), dt))
        out_specs.append(pl.BlockSpec((tm, cols), lambda i, off=off: (i + off, 0)))
    return pl.pallas_call(
        body, grid=(grid_tiles,), in_specs=in_specs, out_specs=out_specs, out_shape=out_shape,
        input_output_aliases={0: 0} if alias_x else {}, scratch_shapes=list(scratch),
        compiler_params=_params(1), name=name)(*row_ins, *full_ins)


def _mem_kv_body(mem_ref, g_ref, w_ref, k_ref, v_ref):
    d = mem_ref.shape[1]
    mn = _rms(mem_ref[...], g_ref[...]).astype(BF16)
    kv = _dot(mn, w_ref[0])
    k_ref[0] = kv[:, :d]
    v_ref[0] = kv[:, d:]


def _mem_kv(mem, g, w_kv):
    depth, d, _ = w_kv.shape
    n_mem = mem.shape[0]
    return pl.pallas_call(
        _mem_kv_body, grid=(depth,),
        in_specs=[_full_spec(mem), _full_spec(g), pl.BlockSpec((1, d, 2 * d), lambda i: (i, 0, 0))],
        out_specs=[pl.BlockSpec((1, n_mem, d), lambda i: (i, 0, 0))] * 2,
        out_shape=[jax.ShapeDtypeStruct((depth, n_mem, d), F32)] * 2,
        compiler_params=_params(1), name="mem_kv")(mem, g, w_kv)


def _conv_in_body(x_ref, g_ref, w_ref, u_ref):
    d = u_ref.shape[1]
    hn = _rms(x_ref[...], g_ref[...]).astype(BF16)
    z = _dot(hn, w_ref[...])
    u_ref[...] = z[:, :d] * jax.nn.sigmoid(z[:, d:])


def _conv_tail(c, lg, lb, wo):
    c = _silu(_ln(c, lg, lb))
    return _dot(c.astype(BF16), wo)


def _conv_prompt_body(x_ref, u_ref, halo_ref, dw8_ref, dwb_ref, lg_ref, lb_ref, wo_ref, o_ref,
                      ext_ref, sh_ref, c_ref, *, taps, halo, rc):
    tm, d = u_ref.shape
    i = pl.program_id(0)
    ext_ref[0:halo, :] = jnp.where(i == 0, 0.0, halo_ref[...])
    ext_ref[halo:halo + tm, :] = u_ref[...]
    ext_ref[halo + tm:, :] = jnp.zeros((ext_ref.shape[0] - halo - tm, d), F32)
    off = halo - (taps - 1)
    n_sh = sh_ref.shape[1]
    for b in range(8):
        sh_ref[b] = ext_ref[pl.ds(off + b, n_sh), :]

    span = rc + 8 * ((taps - 1) // 8)
    lw = min(d, 256)

    def chunk(r, carry):
        r0 = pl.multiple_of(r * rc, rc)
        for l0 in range(0, d, lw):
            acc = jnp.zeros((rc, lw), F32) + dwb_ref[:, l0:l0 + lw]
            for b in range(8):
                win = sh_ref[b, pl.ds(r0, span), l0:l0 + lw]
                for a in range((taps - 1 - b) // 8 + 1):
                    w = jnp.concatenate([dw8_ref[8 * a + b, :, l0:l0 + lw]] * (rc // 8), axis=0)
                    acc = acc + win[8 * a:8 * a + rc] * w
            c_ref[pl.ds(r0, rc), l0:l0 + lw] = acc
        return carry

    lax.fori_loop(0, tm // rc, chunk, 0)
    o_ref[...] = x_ref[...] + _conv_tail(c_ref[...], lg_ref[...], lb_ref[...], wo_ref[...])


def _conv_sample_body(x_ref, ext_ref, dw_ref, dwb_ref, lg_ref, lb_ref, wo_ref, o_ref, c_ref, *, taps):
    bb, ext_len, d = ext_ref.shape
    s = ext_len - (taps - 1)
    for b in range(bb):
        acc = jnp.zeros((s, d), F32) + dwb_ref[...]
        for k in range(taps):
            acc = acc + ext_ref[b, k:k + s, :] * dw_ref[k:k + 1, :]
        c_ref[b * s:(b + 1) * s, :] = acc
    o_ref[...] = x_ref[...] + _conv_tail(c_ref[...], lg_ref[...], lb_ref[...], wo_ref[...])


def _fox_qkv(x_ref, g_ref, wqkv_ref, wfh_ref, wfl_ref, bf_ref):
    d = x_ref.shape[1]
    hn32 = _rms(x_ref[...], g_ref[...])
    qkv = _dot(hn32.astype(BF16), wqkv_ref[...])
    lg = _dot_hi(hn32, wfh_ref[...], wfl_ref[...]) + bf_ref[...]
    return hn32, qkv[:, :d], qkv[:, d:2 * d], qkv[:, 2 * d:], _log_sigmoid(lg)


def _fox_proj_sample_body(x_ref, g_ref, wqkv_ref, wfh_ref, wfl_ref, bf_ref,
                          q_ref, k_ref, v_ref, lf_ref):
    _, q, k, v, logf = _fox_qkv(x_ref, g_ref, wqkv_ref, wfh_ref, wfl_ref, bf_ref)
    q_ref[...] = q.astype(BF16)
    k_ref[...] = k
    v_ref[...] = v
    lf_ref[...] = logf[:, :lf_ref.shape[1]]


def _fox_proj_prompt_body(x_ref, g_ref, wqkv_ref, wfh_ref, wfl_ref, bf_ref, wfth_ref, wftl_ref, bfc_ref,
                          hsel_ref, q_ref, k_ref, v_ref, kb_ref, vb_ref, lf_ref, f_ref, ft_ref, st_ref,
                          crow_ref, ccol_ref):
    tm = x_ref.shape[0]
    heads = lf_ref.shape[1]
    i = pl.program_id(0)

    @pl.when(i == 0)
    def _():
        crow_ref[...] = jnp.zeros_like(crow_ref)
        ccol_ref[...] = jnp.zeros_like(ccol_ref)

    hn32, q, k, v, logf = _fox_qkv(x_ref, g_ref, wqkv_ref, wfh_ref, wfl_ref, bf_ref)
    q_ref[...] = q.astype(BF16)
    k_ref[...] = k
    v_ref[...] = v
    kb_ref[...] = k.astype(BF16)
    vb_ref[...] = v.astype(BF16)
    lf_ref[...] = logf[:, :heads]

    lane = lax.broadcasted_iota(jnp.int32, (tm, LANES), 1)
    logf = jnp.where(lane < heads, logf, 0.0)
    hh, hl = _split2(hn32)
    lgt = _dot_nt(wfth_ref[...], hh) + _dot_nt(wfth_ref[...], hl) + _dot_nt(wftl_ref[...], hh) + bfc_ref[...]
    sub = lax.broadcasted_iota(jnp.int32, (LANES, tm), 0)
    logft = jnp.where(sub < heads, _log_sigmoid(lgt), 0.0)

    r = lax.broadcasted_iota(jnp.int32, (tm, tm), 0)
    c = lax.broadcasted_iota(jnp.int32, (tm, tm), 1)
    lower = jnp.where(c <= r, 1.0, 0.0).astype(BF16)
    upper = jnp.where(r <= c, 1.0, 0.0).astype(BF16)
    ones_r = jnp.ones((8, tm), BF16)
    ones_c = jnp.ones((tm, LANES), BF16)
    a1, a2, a3 = _split3(logf)
    b1, b2, b3 = _split3(logft)
    f_nat = _dot(lower, a1) + _dot(lower, a2) + _dot(lower, a3) + crow_ref[0:1, :]
    f_ref[...] = f_nat
    ft = _dot(b1, upper) + _dot(b2, upper) + _dot(b3, upper)
    ft_ref[...] = ft + jnp.tile(ccol_ref[...], (1, tm // LANES))
    crow_ref[...] = crow_ref[...] + _dot(ones_r, a1) + _dot(ones_r, a2) + _dot(ones_r, a3)
    ccol_ref[...] = ccol_ref[...] + _dot(b1, ones_c) + _dot(b2, ones_c) + _dot(b3, ones_c)

    def max_head_norm2(z):
        zh, zl = _split2(z * z)
        n2 = _dot(zh, hsel_ref[...]) + _dot(zl, hsel_ref[...])
        return jnp.max(jnp.max(n2, axis=1, keepdims=True), axis=0, keepdims=True)

    sub8 = lax.broadcasted_iota(jnp.int32, (8, LANES), 0)
    st_ref[0] = jnp.where(sub8 == 0, max_head_norm2(q),
                          jnp.where(sub8 == 1, max_head_norm2(k),
                                    jnp.where(sub8 == 2, f_nat[0:1, :],
                                              jnp.where(sub8 == 3, f_nat[tm - 1:tm, :], 0.0))))


def _fox_proj_prompt(x_all, weights, *, t_p, tm, heads):
    d = x_all.shape[1]
    tile = lambda cols: pl.BlockSpec((tm, cols), lambda i: (i, 0))
    sds = lambda cols, dt: jax.ShapeDtypeStruct((t_p, cols), dt)
    n = t_p // tm
    return pl.pallas_call(
        _fox_proj_prompt_body, grid=(n,),
        in_specs=[tile(d)] + [_full_spec(w) for w in weights],
        out_specs=[tile(d)] * 5 + [tile(heads), tile(LANES), pl.BlockSpec((LANES, tm), lambda i: (0, i)),
                                   pl.BlockSpec((1, 8, LANES), lambda i: (i, 0, 0))],
        out_shape=[sds(d, BF16), sds(d, F32), sds(d, F32), sds(d, BF16), sds(d, BF16), sds(heads, F32),
                   sds(LANES, F32), jax.ShapeDtypeStruct((LANES, t_p), F32),
                   jax.ShapeDtypeStruct((n, 8, LANES), F32)],
        scratch_shapes=[pltpu.VMEM((8, LANES), F32), pltpu.VMEM((LANES, LANES), F32)],
        compiler_params=_params(1), name="fox_proj_prompt")(x_all, *weights)


def _fox_attn_prompt_body(start_ref, x_ref, q_ref, k_hbm, v_hbm, f_ref, ft_ref, wo_ref, o_ref,
                          kbuf_ref, vbuf_ref, sem, acc_ref, m_ref, ob_ref, *, heads, hd):
    tq = q_ref.shape[0]
    tk = kbuf_ref.shape[1]
    i = pl.program_id(0)
    j0 = start_ref[i]
    n = i - j0 + 1

    def copies(j, slot):
        rows = pl.ds(pl.multiple_of(j * tk, tk), tk)
        return (pltpu.make_async_copy(k_hbm.at[rows], kbuf_ref.at[slot], sem.at[0, slot]),
                pltpu.make_async_copy(v_hbm.at[rows], vbuf_ref.at[slot], sem.at[1, slot]))

    for cp in copies(j0, 0):
        cp.start()
    acc_ref[...] = jnp.zeros_like(acc_ref)
    m_ref[...] = jnp.full_like(m_ref, NEG)

    lane_k = lax.broadcasted_iota(jnp.int32, (tk, LANES), 1)
    keep_lo = jnp.where(lane_k < hd, 1.0, 0.0).astype(BF16)
    keep_hi = jnp.where(lane_k >= hd, 1.0, 0.0).astype(BF16)
    ones_lo = jnp.where(lane_k == hd, 1.0, 0.0).astype(BF16)
    ones_hi = jnp.where(lane_k == 0, 1.0, 0.0).astype(BF16)

    def block(jj, diagonal):
        j = j0 + jj
        slot = lax.rem(jj, 2)

        @pl.when(jj + 1 < n)
        def _():
            for cp in copies(j + 1, 1 - slot):
                cp.start()

        for cp in copies(j, slot):
            cp.wait()
        if diagonal:
            causal = (lax.broadcasted_iota(jnp.int32, (tq, tk), 1)
                      <= lax.broadcasted_iota(jnp.int32, (tq, tk), 0))
        def scores(h):
            sl = slice((h // 2) * LANES, (h // 2 + 1) * LANES)
            kk = kbuf_ref[slot, :, sl] * (keep_lo if h % 2 == 0 else keep_hi)
            fk = (ft_ref[j, h:h + 1, :] - f_ref[0:1, h:h + 1]) * LOG2E
            t = _dot_nt(q_ref[:, sl], kk) - fk
            return jnp.where(causal, t, NEG) if diagonal else t

        t_next = scores(0)
        for h in range(heads):
            t = t_next
            if h + 1 < heads:
                t_next = scores(h + 1)
            sl = slice((h // 2) * LANES, (h // 2 + 1) * LANES)
            vp = vbuf_ref[slot, :, sl]
            vv = vp * keep_lo + ones_lo if h % 2 == 0 else vp * keep_hi + ones_hi
            m_prev = m_ref[h]
            m_new = jnp.maximum(m_prev, jnp.max(t, axis=1, keepdims=True))
            alpha = jnp.exp2(m_prev - m_new)
            pexp = jnp.exp2(t - jnp.tile(m_new, (1, tk // LANES))).astype(BF16)
            acc_ref[h] = alpha * acc_ref[h] + _dot(pexp, vv)
            m_ref[h] = m_new

    def off_diagonal(jj, carry):
        block(jj, False)
        return carry

    lax.fori_loop(0, n - 1, off_diagonal, 0)
    block(n - 1, True)

    lane_q = lax.broadcasted_iota(jnp.int32, (tq, LANES), 1)
    for p in range(heads // 2):
        ae = acc_ref[2 * p]
        ao = acc_ref[2 * p + 1]
        o = jnp.where(lane_q < hd, ae / ae[:, hd:hd + 1], ao / ao[:, 0:1])
        ob_ref[:, p * LANES:(p + 1) * LANES] = o.astype(BF16)
    o_ref[...] = x_ref[...] + _dot(ob_ref[...], wo_ref[...])


SKIP_MARGIN = 100.0


def _fox_band_starts(stats, heads):
    nq = stats.shape[0]
    qmax = jnp.sqrt(stats[:, 0, 0]) * (1.01 / LOG2E)
    kmax = jnp.sqrt(stats[:, 1, 0]) * 1.01
    f_first = stats[:, 2, :heads]
    f_last = stats[:, 3, :heads]
    gap = jnp.max(f_first[:, None, :] - f_last[None, :, :], axis=2)
    bound = gap + qmax[:, None] * (kmax[None, :] + kmax[:, None]) + 0.01
    tile = jnp.arange(nq, dtype=jnp.int32)
    needed = jnp.logical_or(bound >= -SKIP_MARGIN, tile[None, :] == tile[:, None])
    needed = jnp.logical_and(needed, tile[None, :] <= tile[:, None])
    return jnp.argmax(needed, axis=1).astype(jnp.int32)


def _fox_attn_prompt(x_all, q, kb, vb, f_nat, f_t, stats, w_out, *, t_p, tq, heads):
    d = x_all.shape[1]
    hd = d // heads
    nq = t_p // tq
    starts = _fox_band_starts(stats, heads)
    ft3 = jnp.swapaxes(f_t[:heads].reshape(heads, nq, tq), 0, 1)
    grid_spec = pltpu.PrefetchScalarGridSpec(
        num_scalar_prefetch=1, grid=(nq,),
        in_specs=[
            pl.BlockSpec((tq, d), lambda i, st: (i, 0)),
            pl.BlockSpec((tq, d), lambda i, st: (i, 0)),
            pl.BlockSpec(memory_space=pl.ANY),
            pl.BlockSpec(memory_space=pl.ANY),
            pl.BlockSpec((tq, LANES), lambda i, st: (i, 0)),
            pl.BlockSpec((nq, heads, tq), lambda i, st: (0, 0, 0)),
            pl.BlockSpec((d, d), lambda i, st: (0, 0)),
        ],
        out_specs=pl.BlockSpec((tq, d), lambda i, st: (i, 0)),
        scratch_shapes=[pltpu.VMEM((2, tq, d), BF16), pltpu.VMEM((2, tq, d), BF16),
                        pltpu.SemaphoreType.DMA((2, 2)),
                        pltpu.VMEM((heads, tq, LANES), F32), pltpu.VMEM((heads, tq, LANES), F32),
                        pltpu.VMEM((tq, d), BF16)])
    return pl.pallas_call(
        functools.partial(_fox_attn_prompt_body, heads=heads, hd=hd),
        grid_spec=grid_spec, out_shape=jax.ShapeDtypeStruct(x_all.shape, F32),
        input_output_aliases={1: 0}, compiler_params=_params(1), name="fox_attn_prompt",
    )(starts, x_all, q, kb, vb, f_nat, ft3, w_out)


def _fox_attn_sample_body(q_ref, kn_ref, vn_ref, ck_ref, cv_ref, lft_ref, o_ref,
                          qbd_ref, bias_ref, tail_ref, m_ref, l_ref, acc_ref, *, heads, n_chunks):
    s_new, d = q_ref.shape
    hd = d // heads
    rows = heads * s_new
    kc_len = ck_ref.shape[1]
    past = kc_len * n_chunks
    width = lft_ref.shape[2]
    kc = pl.program_id(1)
    row_h = lax.broadcasted_iota(jnp.int32, (rows, d), 0) // s_new
    head_mask = row_h == lax.broadcasted_iota(jnp.int32, (rows, d), 1) // hd

    @pl.when(kc == 0)
    def _():
        qt = jnp.concatenate([q_ref[...].astype(F32)] * heads, axis=0)
        qbd_ref[...] = jnp.where(head_mask, qt, 0.0).astype(BF16)
        lf = lft_ref[0]
        lane = lax.broadcasted_iota(jnp.int32, (heads, width), 1)
        run = lf
        shift = 1
        while shift < width:
            moved = pltpu.roll(run, width - shift, 1)
            run = run + jnp.where(lane + shift < width, moved, 0.0)
            shift *= 2
        rexcl = run - lf
        rexp = jnp.concatenate(
            [jnp.broadcast_to(rexcl[h:h + 1, :], (s_new, width)) for h in range(heads)], axis=0)
        tail = rexp[:, past:]
        tl = lax.broadcasted_iota(jnp.int32, tail.shape, 1)
        tr = lax.broadcasted_iota(jnp.int32, tail.shape, 0) % s_new
        rq = jnp.sum(jnp.where(tl == tr, tail, 0.0), axis=1, keepdims=True)
        for c in range(n_chunks):
            bias_ref[c] = rexp[:, c * kc_len:(c + 1) * kc_len] - rq
        tail_ref[...] = tail - rq
        m_ref[...] = jnp.full_like(m_ref, NEG)
        l_ref[...] = jnp.zeros_like(l_ref)
        acc_ref[...] = jnp.zeros_like(acc_ref)

    def update(s, vmat):
        m_prev = m_ref[...]
        m_new = jnp.maximum(m_prev, jnp.max(s, axis=1, keepdims=True))
        alpha = jnp.exp(m_prev - m_new)
        p = jnp.exp(s - m_new)
        l_ref[...] = alpha * l_ref[...] + jnp.sum(p, axis=1, keepdims=True)
        acc_ref[...] = alpha * acc_ref[...] + _dot(p.astype(BF16), vmat)
        m_ref[...] = m_new

    update(_dot_nt(qbd_ref[...], ck_ref[0]) + bias_ref[kc], cv_ref[0])

    @pl.when(kc == n_chunks - 1)
    def _():
        pad = jnp.zeros((width - past - s_new, d), BF16)
        kn = jnp.concatenate([kn_ref[...].astype(BF16), pad], axis=0)
        vn = jnp.concatenate([vn_ref[...].astype(BF16), pad], axis=0)
        s = _dot_nt(qbd_ref[...], kn) + tail_ref[...]
        tl = lax.broadcasted_iota(jnp.int32, s.shape, 1)
        tr = lax.broadcasted_iota(jnp.int32, s.shape, 0) % s_new
        update(jnp.where(tl <= tr, s, NEG), vn)
        o_full = jnp.where(head_mask, acc_ref[...] / l_ref[...], 0.0)
        out = o_full[0:s_new, :]
        for h in range(1, heads):
            out = out + o_full[h * s_new:(h + 1) * s_new, :]
        o_ref[...] = out.astype(BF16)


def _fox_attn_sample(q_s, k_s, v_s, cache_k, cache_v, lft_all, *, heads, s_new):
    t_s, d = q_s.shape
    n_b = t_s // s_new
    past = cache_k.shape[1]
    n_chunks = 1
    kc_len = past // n_chunks
    width = lft_all.shape[2]
    rows = heads * s_new
    return pl.pallas_call(
        functools.partial(_fox_attn_sample_body, heads=heads, n_chunks=n_chunks),
        grid=(n_b, n_chunks),
        in_specs=[
            pl.BlockSpec((s_new, d), lambda b, c: (b, 0)),
            pl.BlockSpec((s_new, d), lambda b, c: (b, 0)),
            pl.BlockSpec((s_new, d), lambda b, c: (b, 0)),
            pl.BlockSpec((1, kc_len, d), lambda b, c: (b, c, 0)),
            pl.BlockSpec((1, kc_len, d), lambda b, c: (b, c, 0)),
            pl.BlockSpec((1, heads, width), lambda b, c: (b, 0, 0)),
        ],
        out_specs=pl.BlockSpec((s_new, d), lambda b, c: (b, 0)),
        out_shape=jax.ShapeDtypeStruct((t_s, d), BF16),
        scratch_shapes=[pltpu.VMEM((rows, d), BF16), pltpu.VMEM((n_chunks, rows, kc_len), F32),
                        pltpu.VMEM((rows, width - past), F32),
                        pltpu.VMEM((rows, 1), F32), pltpu.VMEM((rows, 1), F32),
                        pltpu.VMEM((rows, d), F32)],
        compiler_params=_params(2), name="fox_attn_sample")(q_s, k_s, v_s, cache_k, cache_v, lft_all)


def _proj_residual_body(x_ref, o_ref_in, w_ref, o_ref):
    o_ref[...] = x_ref[...] + _dot(o_ref_in[...], w_ref[...])


def _sgu_body(x_ref, g_ref, win_ref, lg_ref, lb_ref, ws_ref, bs_ref, wo_ref, o_ref, v_ref, mix_ref,
              *, n_prompt_tiles, groups):
    tm, d = x_ref.shape
    gc = d // groups
    i = pl.program_id(0)
    sel = (i >= n_prompt_tiles).astype(jnp.int32)
    hn = _rms(x_ref[...], g_ref[...]).astype(BF16)
    z = _dot(hn, win_ref[...])
    z = 0.5 * z * (1.0 + lax.erf(z * (2.0 ** -0.5)))
    u = z[:, :d]
    v = _ln(z[:, d:], lg_ref[...], lb_ref[...])
    v_ref[...] = v
    vb = v.astype(BF16)
    for c in range(tm // SGU_CHUNK):
        rs = slice(c * SGU_CHUNK, (c + 1) * SGU_CHUNK)
        for g in range(groups):
            cs = slice(g * gc, (g + 1) * gc)
            mix_ref[rs, cs] = _dot(ws_ref[sel, g], vb[rs, cs])
        mix_ref[rs, :] = mix_ref[rs, :] + bs_ref[sel]
    o_ref[...] = x_ref[...] + _dot((u * mix_ref[...]).astype(BF16), wo_ref[...])


def _xattn_core(q, k_of, v_of, heads):
    d = q.shape[1]
    hd = d // heads
    outs = []
    for h in range(heads):
        s = _dot_nt(q[:, h * hd:(h + 1) * hd], k_of(h))
        p = jnp.exp(s - jnp.max(s, axis=1, keepdims=True))
        p = p / jnp.sum(p, axis=1, keepdims=True)
        outs.append(_dot(p.astype(BF16), v_of(h)).astype(BF16))
    return outs


def _xattn_prompt_body(x_ref, g_ref, wq_ref, k_ref, v_ref, wo_ref, o_ref, *, heads, layer):
    d = x_ref.shape[1]
    hd = d // heads
    x = x_ref[...]
    q = _dot(_rms(x, g_ref[...]).astype(BF16), wq_ref[...]).astype(BF16)
    k = k_ref[layer].astype(BF16)
    v = v_ref[layer].astype(BF16)
    outs = _xattn_core(q, lambda h: k[:, h * hd:(h + 1) * hd], lambda h: v[:, h * hd:(h + 1) * hd], heads)
    y = x
    for h, oh in enumerate(outs):
        y = y + _dot(oh, wo_ref[h * hd:(h + 1) * hd, :])
    o_ref[...] = y


def _xattn_sample_body(x_ref, g_ref, wq_ref, k_ref, v_ref, wo_ref, o_ref, ob_ref, *, heads, s_new):
    d = x_ref.shape[1]
    hd = d // heads
    bb = k_ref.shape[1]
    x = x_ref[...]
    q = _dot(_rms(x, g_ref[...]).astype(BF16), wq_ref[...]).astype(BF16)
    for b in range(bb):
        rs = slice(b * s_new, (b + 1) * s_new)
        outs = _xattn_core(q[rs, :], lambda h, b=b: k_ref[0, b, :, h, :].astype(BF16),
                           lambda h, b=b: v_ref[0, b, :, h, :].astype(BF16), heads)
        for h, oh in enumerate(outs):
            ob_ref[rs, h * hd:(h + 1) * hd] = oh
    o_ref[...] = x + _dot(ob_ref[...], wo_ref[...])


def _ffn_body(x_ref, g_ref, wg_ref, wu_ref, wd_ref, o_ref, hn_ref, acc_ref):
    f = pl.program_id(1)

    @pl.when(f == 0)
    def _():
        hn_ref[...] = _rms(x_ref[...], g_ref[...]).astype(BF16)
        acc_ref[...] = jnp.zeros_like(acc_ref)

    hn = hn_ref[...]
    a = _silu(_dot(hn, wg_ref[...])) * _dot(hn, wu_ref[...])
    acc_ref[...] += _dot(a.astype(BF16), wd_ref[...])

    @pl.when(f == pl.num_programs(1) - 1)
    def _():
        o_ref[...] = x_ref[...] + acc_ref[...]


def _ffn(x_all, g, w_gu, w_down, *, tm, tf):
    m, d = x_all.shape
    ff = w_down.shape[0]
    nf = ff // tf
    return pl.pallas_call(
        _ffn_body, grid=(m // tm, nf),
        in_specs=[pl.BlockSpec((tm, d), lambda i, f: (i, 0)),
                  _full_spec(g),
                  pl.BlockSpec((d, tf), lambda i, f: (0, f)),
                  pl.BlockSpec((d, tf), lambda i, f, nf=nf: (0, nf + f)),
                  pl.BlockSpec((tf, d), lambda i, f: (f, 0))],
        out_specs=pl.BlockSpec((tm, d), lambda i, f: (i, 0)),
        out_shape=jax.ShapeDtypeStruct((m, d), F32),
        input_output_aliases={0: 0},
        scratch_shapes=[pltpu.VMEM((tm, d), BF16), pltpu.VMEM((tm, d), F32)],
        compiler_params=_params(2), name="ffn")(x_all, g, w_gu, w_gu, w_down)


def _router_body(x_ref, g_ref, rh_ref, rl_ref, idx_ref, gate_ref, rank_ref, cnt_ref, carry_ref, *, n_experts):
    tm = x_ref.shape[0]

    @pl.when(pl.program_id(0) == 0)
    def _():
        carry_ref[...] = jnp.zeros_like(carry_ref)

    hn32 = _rms(x_ref[...], g_ref[...])
    logits = _dot_hi(hn32, rh_ref[...], rl_ref[...])
    lane = lax.broadcasted_iota(jnp.int32, logits.shape, 1)
    lane_f = lane.astype(F32)
    big = float(LANES)
    masked = jnp.where(lane < n_experts, logits, -jnp.inf)
    m1 = jnp.max(masked, axis=1, keepdims=True)
    i1 = jnp.min(jnp.where(masked == m1, lane_f, big), axis=1, keepdims=True)
    masked2 = jnp.where(lane_f == i1, -jnp.inf, masked)
    m2 = jnp.max(masked2, axis=1, keepdims=True)
    i2 = jnp.min(jnp.where(masked2 == m2, lane_f, big), axis=1, keepdims=True)
    e = jnp.exp(m2 - m1)
    w1 = 1.0 / (1.0 + e)
    w2 = e * w1
    idx_ref[...] = jnp.where(lane == 0, i1, jnp.where(lane == 1, i2, 0.0)).astype(jnp.int32)
    gate_ref[...] = jnp.where(lane == 0, w1, jnp.where(lane == 1, w2, 0.0))
    hit1 = lane_f == i1
    hit2 = lane_f == i2
    onehot = jnp.where(jnp.logical_or(hit1, hit2), 1.0, 0.0).astype(BF16)
    r = lax.broadcasted_iota(jnp.int32, (tm, tm), 0)
    c = lax.broadcasted_iota(jnp.int32, (tm, tm), 1)
    before = jnp.where(c < r, 1.0, 0.0).astype(BF16)
    base = _dot(before, onehot) + carry_ref[0:1, :]
    r1 = jnp.sum(jnp.where(hit1, base, 0.0), axis=1, keepdims=True)
    r2 = jnp.sum(jnp.where(hit2, base, 0.0), axis=1, keepdims=True)
    rank_ref[...] = jnp.where(lane == 0, r1, jnp.where(lane == 1, r2, 0.0)).astype(jnp.int32)
    carry_ref[...] = carry_ref[...] + _dot(jnp.ones((8, tm), BF16), onehot)
    cnt_ref[...] = carry_ref[...]


def _moe_router(x_all, g, r_hi, r_lo, *, tm, n_experts):
    m, d = x_all.shape
    tile = pl.BlockSpec((tm, LANES), lambda i: (i, 0))
    return pl.pallas_call(
        functools.partial(_router_body, n_experts=n_experts), grid=(m // tm,),
        in_specs=[pl.BlockSpec((tm, d), lambda i: (i, 0)), _full_spec(g), _full_spec(r_hi), _full_spec(r_lo)],
        out_specs=[tile, tile, tile, pl.BlockSpec((8, LANES), lambda i: (0, 0))],
        out_shape=[jax.ShapeDtypeStruct((m, LANES), jnp.int32), jax.ShapeDtypeStruct((m, LANES), F32),
                   jax.ShapeDtypeStruct((m, LANES), jnp.int32), jax.ShapeDtypeStruct((8, LANES), F32)],
        scratch_shapes=[pltpu.VMEM((8, LANES), F32)],
        compiler_params=_params(1), name="moe_router")(x_all, g, r_hi, r_lo)


def _row_copies(src_ref, dst_ref, idx_ref, base, sem, n, *, gather, stride=1, offset=0):
    def copy(j):
        row = idx_ref[base + j * stride + offset]
        if gather:
            return pltpu.make_async_copy(src_ref.at[pl.ds(row, 1)], dst_ref.at[pl.ds(j, 1)], sem)
        return pltpu.make_async_copy(src_ref.at[pl.ds(j, 1)], dst_ref.at[pl.ds(row, 1)], sem)

    def wait(j, c):
        copy(j).wait()
        return c

    for j in range(n):
        copy(j).start()
    lax.fori_loop(0, n, wait, 0, unroll=8)


def _moe_dispatch_body(dest_ref, tend_ref, x_ref, g_ref, out_hbm, hbuf_ref, sem0, sem1):
    tm = x_ref.shape[0]
    base = pl.program_id(0) * (tm * TOP_K)

    @pl.when(pl.program_id(0) == 0)
    def _():
        hbuf_ref[...] = jnp.zeros_like(hbuf_ref)
        n_experts = tend_ref.shape[0]
        n_used = tend_ref[n_experts - 1]
        r_max = out_hbm.shape[0] // tm

        def fill(tile):
            cp = pltpu.make_async_copy(hbuf_ref, out_hbm.at[pl.ds(pl.multiple_of(tile * tm, tm), tm)], sem0)
            cp.start()
            cp.wait()

        for e in range(n_experts):
            fill(jnp.maximum(tend_ref[e] - 1, 0))
        for k in range(n_experts - 1):
            @pl.when(n_used + k < r_max)
            def _(k=k):
                fill(n_used + k)

    hbuf_ref[...] = _rms(x_ref[...], g_ref[...])
    _row_copies(hbuf_ref, out_hbm, dest_ref, base, sem0, tm, gather=False, stride=TOP_K, offset=0)
    _row_copies(hbuf_ref, out_hbm, dest_ref, base, sem1, tm, gather=False, stride=TOP_K, offset=1)


def _moe_dispatch(x_all, g, dest, tile_end, *, tm, rows_sorted):
    m, d = x_all.shape
    grid_spec = pltpu.PrefetchScalarGridSpec(
        num_scalar_prefetch=2, grid=(m // tm,),
        in_specs=[pl.BlockSpec((tm, d), lambda i, dest, tend: (i, 0)),
                  pl.BlockSpec((1, d), lambda i, dest, tend: (0, 0))],
        out_specs=pl.BlockSpec(memory_space=pl.ANY),
        scratch_shapes=[pltpu.VMEM((tm, d), F32), pltpu.SemaphoreType.DMA(()), pltpu.SemaphoreType.DMA(())])
    return pl.pallas_call(
        _moe_dispatch_body, grid_spec=grid_spec, out_shape=jax.ShapeDtypeStruct((rows_sorted, d), F32),
        compiler_params=_params(1), name="moe_dispatch")(dest, tile_end, x_all, g)


def _moe_ffn_body(te_ref, nt_ref, hs_ref, wg_ref, wu_ref, wd_ref, o_ref, hb_ref, acc_ref):
    r = pl.program_id(0)
    f = pl.program_id(1)
    valid = r < nt_ref[0]

    @pl.when(jnp.logical_and(valid, f == 0))
    def _():
        hb_ref[...] = hs_ref[...].astype(BF16)
        acc_ref[...] = jnp.zeros_like(acc_ref)

    @pl.when(valid)
    def _():
        h = hb_ref[...]
        a = _silu(_dot(h, wg_ref[0, 0])) * _dot(h, wu_ref[0, 0])
        acc_ref[...] += _dot(a.astype(BF16), wd_ref[0, 0])

    last = f == pl.num_programs(1) - 1

    @pl.when(jnp.logical_and(valid, last))
    def _():
        o_ref[...] = acc_ref[...]

    @pl.when(jnp.logical_and(jnp.logical_not(valid), last))
    def _():
        o_ref[...] = jnp.zeros_like(o_ref)


def _moe_ffn(hn_sorted, tile_expert, n_tiles, w_gu, w_down, *, layer, tm, tf):
    rows, d = hn_sorted.shape
    d_exp = w_down.shape[2]
    nf = d_exp // tf

    def fe(r, f, nt):
        return jnp.where(r < nt[0], f, nf - 1)

    grid_spec = pltpu.PrefetchScalarGridSpec(
        num_scalar_prefetch=2, grid=(rows // tm, nf),
        in_specs=[
            pl.BlockSpec((tm, d), lambda r, f, te, nt: (jnp.minimum(r, nt[0] - 1), 0)),
            pl.BlockSpec((1, 1, d, tf), lambda r, f, te, nt: (layer, te[r], 0, fe(r, f, nt))),
            pl.BlockSpec((1, 1, d, tf), lambda r, f, te, nt: (layer, te[r], 0, nf + fe(r, f, nt))),
            pl.BlockSpec((1, 1, tf, d), lambda r, f, te, nt: (layer, te[r], fe(r, f, nt), 0)),
        ],
        out_specs=pl.BlockSpec((tm, d), lambda r, f, te, nt: (r, 0)),
        scratch_shapes=[pltpu.VMEM((tm, d), BF16), pltpu.VMEM((tm, d), F32)])
    return pl.pallas_call(
        _moe_ffn_body, grid_spec=grid_spec, out_shape=jax.ShapeDtypeStruct((rows, d), F32),
        compiler_params=_params(2), name="moe_ffn")(tile_expert, n_tiles, hn_sorted, w_gu, w_gu, w_down)


def _moe_combine_nonorm_body(dest_ref, x_ref, gate_ref, y_hbm, o_ref, b0_ref, b1_ref, sem0, sem1):
    _moe_combine_body(dest_ref, x_ref, gate_ref, y_hbm, None, o_ref, b0_ref, b1_ref, sem0, sem1)


def _moe_combine_body(dest_ref, x_ref, gate_ref, y_hbm, gn_ref, o_ref, b0_ref, b1_ref, sem0, sem1):
    tm = x_ref.shape[0]
    base = pl.program_id(0) * (tm * TOP_K)
    _row_copies(y_hbm, b0_ref, dest_ref, base, sem0, tm, gather=True, stride=TOP_K, offset=0)
    _row_copies(y_hbm, b1_ref, dest_ref, base, sem1, tm, gather=True, stride=TOP_K, offset=1)
    gate = gate_ref[...]
    y = x_ref[...] + gate[:, 0:1] * b0_ref[...] + gate[:, 1:2] * b1_ref[...]
    o_ref[...] = y if gn_ref is None else _rms(y, gn_ref[...])


def _moe_combine(x_all, gates, dest, y_sorted, *, tm, final_gain=None):
    m, d = x_all.shape
    body = _moe_combine_body
    extra_specs, extra_args = [], []
    if final_gain is None:
        body = _moe_combine_nonorm_body
    else:
        extra_specs, extra_args = [pl.BlockSpec((1, d), lambda i, dest: (0, 0))], [final_gain]
    grid_spec = pltpu.PrefetchScalarGridSpec(
        num_scalar_prefetch=1, grid=(m // tm,),
        in_specs=[pl.BlockSpec((tm, d), lambda i, dest: (i, 0)),
                  pl.BlockSpec((tm, LANES), lambda i, dest: (i, 0)),
                  pl.BlockSpec(memory_space=pl.ANY)] + extra_specs,
        out_specs=pl.BlockSpec((tm, d), lambda i, dest: (i, 0)),
        scratch_shapes=[pltpu.VMEM((tm, d), F32), pltpu.VMEM((tm, d), F32),
                        pltpu.SemaphoreType.DMA(()), pltpu.SemaphoreType.DMA(())])
    return pl.pallas_call(
        body, grid_spec=grid_spec, out_shape=jax.ShapeDtypeStruct((m, d), F32),
        input_output_aliases={1: 0}, compiler_params=_params(1), name="moe_combine",
    )(dest, x_all, gates, y_sorted, *extra_args)


def _moe_plan(top_i, rank, counts, tm, r_max):
    n_experts = counts.shape[0]
    tiles = (counts + tm - 1) // tm
    tile_end = jnp.cumsum(tiles)
    row_start = (tile_end - tiles) * tm
    onehot = top_i[:, :, None] == jnp.arange(n_experts, dtype=jnp.int32)[None, None, :]
    dest = (jnp.sum(jnp.where(onehot, row_start[None, None, :], 0), axis=2) + rank).astype(jnp.int32)
    n_tiles = tile_end[-1:].astype(jnp.int32)
    tile_id = jnp.arange(r_max, dtype=jnp.int32)
    tile_expert = jnp.minimum(jnp.sum((tile_id[:, None] >= tile_end[None, :]).astype(jnp.int32), axis=1),
                              n_experts - 1).astype(jnp.int32)
    return dest.reshape(-1), n_tiles, tile_expert, tile_end.astype(jnp.int32)


def _final_norm_body(x_ref, g_ref, o_ref):
    o_ref[...] = _rms(x_ref[...], g_ref[...])


def _pick(n, cands):
    for c in cands:
        if n % c == 0:
            return c
    raise ValueError(f"no tile in {cands} divides {n}")


def kernel(x_prompt, x_sample, mem_prompt, cache_mem_k, cache_mem_v, state_conv, cache_fox_k, cache_fox_v, cache_fox_logf, norm_mix, norm_mem_q, norm_ffn, norm_mem, norm_final, conv_w_in, conv_dw, conv_dw_b, conv_ln_g, conv_ln_b, conv_w_out, fox_w_in, fox_b_f, fox_w_out, sgu_w_in, sgu_ln_g, sgu_ln_b, sgu_w_s, sgu_b_s, sgu_w_out, xa_w_q, xa_w_kv, xa_w_o, ffn_w_gu, ffn_w_down, moe_router, moe_w_gu, moe_w_down):
    batch, t_p, d = x_prompt.shape
    n_b, s_new, _ = x_sample.shape
    assert batch == 1
    t_s = n_b * s_new
    m = t_p + t_s
    depth = norm_mix.shape[0]
    n_mem = mem_prompt.shape[1]
    xa_heads = cache_mem_k.shape[3]
    taps = conv_dw.shape[1]
    fox_heads = fox_b_f.shape[1]
    past = cache_fox_k.shape[2]
    groups = sgu_w_s.shape[1]
    n_experts = moe_router.shape[2]
    d_exp = moe_w_down.shape[2]
    d_ff = ffn_w_down.shape[1]

    tm = _pick(int(np.gcd(t_p, t_s)), (512, 256, 128))
    n_pt = t_p // tm
    n_st = t_s // tm
    n_t = n_pt + n_st
    halo = 32
    assert taps - 1 <= halo and tm % halo == 0 and tm % SGU_CHUNK == 0 and SGU_CHUNK % s_new == 0
    assert fox_heads % 2 == 0 and d // fox_heads * 2 == LANES and fox_heads <= LANES
    assert past % SGU_CHUNK == 0 and s_new <= SGU_BLOCK

    row = lambda a: a.reshape(1, -1)
    x_all = jnp.concatenate([x_prompt[0], x_sample.reshape(t_s, d)], axis=0)

    mem_k, mem_v = _mem_kv(mem_prompt[0], row(norm_mem), xa_w_kv.astype(BF16))
    moe_w_gu_b = moe_w_gu.astype(BF16)
    moe_w_down_b = moe_w_down.astype(BF16)

    conv_p, conv_s = [], []
    fox_out = []
    sgu_v = []
    for i in range(depth):
        kind = i % 3
        j = i // 3
        g_mix = row(norm_mix[i])
        if kind == 0:
            (u_all,) = _row_call(
                _conv_in_body, grid_tiles=n_t, tm=tm, first_tile=0, row_ins=[x_all],
                full_ins=[g_mix, conv_w_in[j].astype(BF16)], outs=[(m, d, F32, 0)], alias_x=False,
                name="conv_in")
            dw = jnp.concatenate([conv_dw[j], jnp.zeros((halo - taps, d), F32)], axis=0)
            tail_w = [dw, row(conv_dw_b[j]), row(conv_ln_g[j]), row(conv_ln_b[j]), conv_w_out[j].astype(BF16)]
            hb = tm // halo
            prompt_w = [jnp.broadcast_to(conv_dw[j][:, None, :], (taps, 8, d))] + tail_w[1:]
            x_all = pl.pallas_call(
                functools.partial(_conv_prompt_body, taps=taps, halo=halo, rc=32), grid=(n_pt,),
                in_specs=[pl.BlockSpec((tm, d), lambda t: (t, 0)),
                          pl.BlockSpec((tm, d), lambda t: (t, 0)),
                          pl.BlockSpec((halo, d), lambda t, hb=hb: (jnp.maximum(t * hb - 1, 0), 0))]
                         + [_full_spec(a) for a in prompt_w],
                out_specs=pl.BlockSpec((tm, d), lambda t: (t, 0)),
                out_shape=jax.ShapeDtypeStruct((m, d), F32), input_output_aliases={0: 0},
                scratch_shapes=[pltpu.VMEM((tm + halo + 8, d), F32),
                                pltpu.VMEM((8, tm + 8 * ((taps - 1) // 8), d), F32),
                                pltpu.VMEM((tm, d), F32)],
                compiler_params=_params(1), name="conv_prompt")(x_all, u_all, u_all, *prompt_w)
            u_s = u_all[t_p:].reshape(n_b, s_new, d)
            ext_s = jnp.concatenate([state_conv[j], u_s], axis=1)
            bb = _pick(n_b, (8, 4, 2, 1))
            rows_s = bb * s_new
            x_all = pl.pallas_call(
                functools.partial(_conv_sample_body, taps=taps), grid=(n_b // bb,),
                in_specs=[pl.BlockSpec((rows_s, d), lambda t, o=t_p // rows_s: (t + o, 0)),
                          pl.BlockSpec((bb, taps - 1 + s_new, d), lambda t: (t, 0, 0))]
                         + [_full_spec(a) for a in tail_w],
                out_specs=pl.BlockSpec((rows_s, d), lambda t, o=t_p // rows_s: (t + o, 0)),
                out_shape=jax.ShapeDtypeStruct((m, d), F32), input_output_aliases={0: 0},
                scratch_shapes=[pltpu.VMEM((rows_s, d), F32)],
                compiler_params=_params(1), name="conv_sample")(x_all, ext_s, *tail_w)
            conv_p.append(u_all[t_p - (taps - 1):t_p][None])
            conv_s.append(ext_s[:, s_new:])
        elif kind == 1:
            hd = d // fox_heads
            w_in = fox_w_in[j]
            w_qkv = jnp.concatenate([w_in[:, :d] * (hd ** -0.5), w_in[:, d:3 * d]], axis=1).astype(BF16)
            w_qkv_p = jnp.concatenate([w_in[:, :d] * (hd ** -0.5 * LOG2E), w_in[:, d:3 * d]],
                                      axis=1).astype(BF16)
            w_f = jnp.concatenate([w_in[:, 3 * d:], jnp.zeros((d, LANES - fox_heads), F32)], axis=1)
            wf_hi = w_f.astype(BF16)
            wf_lo = (w_f - wf_hi.astype(F32)).astype(BF16)
            b_f = jnp.concatenate([fox_b_f[j], jnp.zeros((LANES - fox_heads,), F32)])
            b_row = row(b_f)
            ta = min(tm, FOX_TILE)
            b_col = jnp.broadcast_to(b_f[:, None], (LANES, ta))
            head_sel = (jnp.arange(d)[:, None] // hd == jnp.arange(LANES)[None, :]).astype(BF16)
            proj_w = [g_mix, w_qkv, wf_hi, wf_lo, b_row]
            q_p, k_p, v_p, kb_p, vb_p, lf_p, f_nat, f_t, stats = _fox_proj_prompt(
                x_all, [g_mix, w_qkv_p, wf_hi, wf_lo, b_row, wf_hi.T, wf_lo.T, b_col, head_sel],
                t_p=t_p, tm=ta, heads=fox_heads)
            w_out = fox_w_out[j].astype(BF16)
            x_all = _fox_attn_prompt(x_all, q_p, kb_p, vb_p, f_nat, f_t, stats, w_out, t_p=t_p, tq=ta,
                                     heads=fox_heads)
            q_s, k_s, v_s, lf_s = _row_call(
                _fox_proj_sample_body, grid_tiles=n_st, tm=tm, first_tile=n_pt, row_ins=[x_all],
                full_ins=proj_w,
                outs=[(t_s, d, BF16, 0), (t_s, d, F32, 0), (t_s, d, F32, 0), (t_s, fox_heads, F32, 0)],
                alias_x=False, name="fox_proj_sample")
            width = past + LANES
            lft_all = jnp.concatenate(
                [jnp.swapaxes(cache_fox_logf[j], 1, 2),
                 jnp.swapaxes(lf_s.reshape(n_b, s_new, fox_heads), 1, 2),
                 jnp.zeros((n_b, fox_heads, width - past - s_new), F32)], axis=2)
            o_s = _fox_attn_sample(q_s, k_s, v_s, cache_fox_k[j].astype(BF16).reshape(n_b, past, d),
                                   cache_fox_v[j].astype(BF16).reshape(n_b, past, d), lft_all,
                                   heads=fox_heads, s_new=s_new)
            x_all = pl.pallas_call(
                _proj_residual_body, grid=(n_st,),
                in_specs=[pl.BlockSpec((tm, d), lambda t, o=n_pt: (t + o, 0)),
                          pl.BlockSpec((tm, d), lambda t: (t, 0)), _full_spec(w_out)],
                out_specs=pl.BlockSpec((tm, d), lambda t, o=n_pt: (t + o, 0)),
                out_shape=jax.ShapeDtypeStruct((m, d), F32), input_output_aliases={0: 0},
                compiler_params=_params(1), name="fox_out_sample")(x_all, o_s, w_out)
            fox_out.append((k_p, v_p, lf_p, k_s, v_s, lf_s))
        else:
            ws = sgu_w_s[j]
            pos = jnp.arange(SGU_CHUNK)
            blk_mask = (pos[None, :] // SGU_BLOCK) <= (pos[:, None] // SGU_BLOCK)
            ws_p = jnp.where(blk_mask[None], ws, 0.0)
            reps = SGU_CHUNK // s_new
            eye = jnp.eye(reps, dtype=F32)
            ws_s = jnp.einsum("ab,gij->gaibj", eye, ws_p[:, :s_new, :s_new]).reshape(
                groups, SGU_CHUNK, SGU_CHUNK)
            ws_all = jnp.stack([ws_p, ws_s]).astype(BF16)
            gc = d // groups
            bs_p = jnp.repeat(sgu_b_s[j].T, gc, axis=1)
            bs_s = jnp.tile(bs_p[:s_new], (reps, 1))
            bs_all = jnp.stack([bs_p, bs_s])
            x_all, v_all = _row_call(
                functools.partial(_sgu_body, n_prompt_tiles=n_pt, groups=groups),
                grid_tiles=n_t, tm=tm, first_tile=0, row_ins=[x_all],
                full_ins=[g_mix, sgu_w_in[j].astype(BF16), row(sgu_ln_g[j]), row(sgu_ln_b[j]),
                          ws_all, bs_all, sgu_w_out[j].astype(BF16)],
                outs=[(m, d, F32, 0), (m, d, F32, 0)], scratch=[pltpu.VMEM((tm, d), F32)], name="sgu")
            sgu_v.append(v_all[t_p:].reshape(n_b, s_new, d))

        xa_hd = d // xa_heads
        wq = (xa_w_q[i] * (xa_hd ** -0.5)).astype(BF16)
        wo = xa_w_o[i].astype(BF16)
        g_q = row(norm_mem_q[i])
        (x_all,) = _row_call(
            functools.partial(_xattn_prompt_body, heads=xa_heads, layer=i),
            grid_tiles=n_pt, tm=tm, first_tile=0, row_ins=[x_all],
            full_ins=[g_q, wq, mem_k, mem_v, wo], outs=[(m, d, F32, 0)], name="xattn_prompt")
        bb = _pick(n_b, (4, 2, 1))
        rows_s = bb * s_new
        x_all = pl.pallas_call(
            functools.partial(_xattn_sample_body, heads=xa_heads, s_new=s_new), grid=(n_b // bb,),
            in_specs=[pl.BlockSpec((rows_s, d), lambda t, o=t_p // rows_s: (t + o, 0)),
                      _full_spec(g_q), _full_spec(wq),
                      pl.BlockSpec((1, bb, n_mem, xa_heads, xa_hd), lambda t, i=i: (i, t, 0, 0, 0)),
                      pl.BlockSpec((1, bb, n_mem, xa_heads, xa_hd), lambda t, i=i: (i, t, 0, 0, 0)),
                      _full_spec(wo)],
            out_specs=pl.BlockSpec((rows_s, d), lambda t, o=t_p // rows_s: (t + o, 0)),
            out_shape=jax.ShapeDtypeStruct((m, d), F32), input_output_aliases={0: 0},
            scratch_shapes=[pltpu.VMEM((rows_s, d), BF16)],
            compiler_params=_params(1), name="xattn_sample")(x_all, g_q, wq, cache_mem_k, cache_mem_v, wo)

        g_f = row(norm_ffn[i])
        c = i // 2
        if i % 2 == 0:
            tf = _pick(d_ff, (1408, 1024, 512, 256, 128))
            x_all = _ffn(x_all, g_f, ffn_w_gu[c].astype(BF16), ffn_w_down[c].astype(BF16), tm=tm, tf=tf)
        else:
            router = jnp.concatenate([moe_router[c], jnp.zeros((d, LANES - n_experts), F32)], axis=1)
            r_hi = router.astype(BF16)
            r_lo = (router - r_hi.astype(F32)).astype(BF16)
            idx, gates, rank, counts = _moe_router(x_all, g_f, r_hi, r_lo, tm=tm, n_experts=n_experts)
            r_max = (TOP_K * m + n_experts * (tm - 1)) // tm
            dest, n_tiles, tile_expert, tile_end = _moe_plan(
                idx[:, :TOP_K], rank[:, :TOP_K], counts[0, :n_experts].astype(jnp.int32), tm, r_max)
            hn_sorted = _moe_dispatch(x_all, g_f, dest, tile_end, tm=tm, rows_sorted=r_max * tm)
            tf = _pick(d_exp, (1792, 1024, 512, 256, 128))
            y_sorted = _moe_ffn(hn_sorted, tile_expert, n_tiles, moe_w_gu_b, moe_w_down_b, layer=c, tm=tm, tf=tf)
            last = i == depth - 1
            x_all = _moe_combine(x_all, gates, dest, y_sorted, tm=tm,
                                 final_gain=row(norm_final) if last else None)

    if depth % 2 == 0:
        y_all = x_all
    else:
        (y_all,) = _row_call(_final_norm_body, grid_tiles=n_t, tm=tm, first_tile=0, row_ins=[x_all],
                             full_ins=[row(norm_final)], outs=[(m, d, F32, 0)], alias_x=False,
                             name="final_norm")
    y_prompt = y_all[:t_p][None]
    y_sample = y_all[t_p:].reshape(n_b, s_new, d)
    mem_shape = (depth, 1, n_mem, xa_heads, d // xa_heads)
    fk_p, fv_p, fl_p, fk_s, fv_s, fl_s = zip(*fox_out)
    hshape_p = (1, t_p, fox_heads, d // fox_heads)
    hshape_s = (n_b, s_new, fox_heads, d // fox_heads)
    return (y_prompt, y_sample, mem_k.reshape(mem_shape), mem_v.reshape(mem_shape),
            jnp.stack(conv_p), jnp.stack(conv_s),
            jnp.stack([a.reshape(hshape_p) for a in fk_p]), jnp.stack([a.reshape(hshape_p) for a in fv_p]),
            jnp.stack([a.reshape(1, t_p, fox_heads) for a in fl_p]),
            jnp.stack([a.reshape(hshape_s) for a in fk_s]), jnp.stack([a.reshape(hshape_s) for a in fv_s]),
            jnp.stack([a.reshape(n_b, s_new, fox_heads) for a in fl_s]),
            jnp.stack(sgu_v))
```

```python
import functools

import numpy as np
import jax
import jax.numpy as jnp
from jax import lax
from jax.experimental import pallas as pl
from jax.experimental.pallas import tpu as pltpu

F32 = jnp.float32
BF16 = jnp.bfloat16
RMS_EPS = 1e-6
LN_EPS = 1e-5
SGU_CHUNK = 128
SGU_BLOCK = 64
TOP_K = 2
LANES = 128
NEG = -1e30
LOG2E = 1.4426950408889634
VMEM_LIMIT = 52 * 1024 * 1024
FOX_TILE = 512


def _params(n_axes, vmem=VMEM_LIMIT):
    return pltpu.CompilerParams(dimension_semantics=("arbitrary",) * n_axes, vmem_limit_bytes=vmem)


def _dot(a, b):
    return jnp.dot(a, b, preferred_element_type=F32)


def _dot_nt(a, b):
    return lax.dot_general(a, b, (((1,), (1,)), ((), ())), preferred_element_type=F32)


def _rms(x, g):
    return x * lax.rsqrt(jnp.mean(x * x, axis=-1, keepdims=True) + RMS_EPS) * g


def _ln(x, g, b):
    mu = jnp.mean(x, axis=-1, keepdims=True)
    xc = x - mu
    var = jnp.mean(xc * xc, axis=-1, keepdims=True)
    return xc * lax.rsqrt(var + LN_EPS) * g + b


def _silu(x):
    return x * jax.nn.sigmoid(x)


def _log_sigmoid(x):
    return jnp.minimum(x, 0.0) - jnp.log(1.0 + jnp.exp(-jnp.abs(x)))


def _split3(x):
    a = x.astype(BF16)
    r = x - a.astype(F32)
    b = r.astype(BF16)
    c = (r - b.astype(F32)).astype(BF16)
    return a, b, c


def _split2(x):
    a = x.astype(BF16)
    return a, (x - a.astype(F32)).astype(BF16)


def _dot_hi(x32, w_hi, w_lo):
    xh, xl = _split2(x32)
    return _dot(xh, w_hi) + _dot(xl, w_hi) + _dot(xh, w_lo)


def _full_spec(a):
    nd = a.ndim
    return pl.BlockSpec(a.shape, lambda *_, nd=nd: (0,) * nd)


def _row_call(body, *, grid_tiles, tm, first_tile, row_ins, full_ins, outs, alias_x=True,
              scratch=(), name=None):
    in_specs = [pl.BlockSpec((tm, a.shape[1]), lambda i, ft=first_tile: (i + ft, 0)) for a in row_ins]
    in_specs += [_full_spec(a) for a in full_ins]
    out_shape, out_specs = [], []
    for rows, cols, dt, off in outs:
        out_shape.append(jax.ShapeDtypeStruct((rows, cols), dt))
        out_specs.append(pl.BlockSpec((tm, cols), lambda i, off=off: (i + off, 0)))
    return pl.pallas_call(
        body, grid=(grid_tiles,), in_specs=in_specs, out_specs=out_specs, out_shape=out_shape,
        input_output_aliases={0: 0} if alias_x else {}, scratch_shapes=list(scratch),
        compiler_params=_params(1), name=name)(*row_ins, *full_ins)


def _mem_kv_body(mem_ref, g_ref, w_ref, k_ref, v_ref):
    d = mem_ref.shape[1]
    mn = _rms(mem_ref[...], g_ref[...]).astype(BF16)
    kv = _dot(mn, w_ref[0])
    k_ref[0] = kv[:, :d]
    v_ref[0] = kv[:, d:]


def _mem_kv(mem, g, w_kv):
    depth, d, _ = w_kv.shape
    n_mem = mem.shape[0]
    return pl.pallas_call(
        _mem_kv_body, grid=(depth,),
        in_specs=[_full_spec(mem), _full_spec(g), pl.BlockSpec((1, d, 2 * d), lambda i: (i, 0, 0))],
        out_specs=[pl.BlockSpec((1, n_mem, d), lambda i: (i, 0, 0))] * 2,
        out_shape=[jax.ShapeDtypeStruct((depth, n_mem, d), F32)] * 2,
        compiler_params=_params(1), name="mem_kv")(mem, g, w_kv)


def _conv_in_body(x_ref, g_ref, w_ref, u_ref):
    d = u_ref.shape[1]
    hn = _rms(x_ref[...], g_ref[...]).astype(BF16)
    z = _dot(hn, w_ref[...])
    u_ref[...] = z[:, :d] * jax.nn.sigmoid(z[:, d:])


def _conv_tail(c, lg, lb, wo):
    c = _silu(_ln(c, lg, lb))
    return _dot(c.astype(BF16), wo)


def _conv_prompt_body(x_ref, u_ref, halo_ref, dw8_ref, dwb_ref, lg_ref, lb_ref, wo_ref, o_ref,
                      ext_ref, sh_ref, c_ref, *, taps, halo, rc):
    tm, d = u_ref.shape
    i = pl.program_id(0)
    ext_ref[0:halo, :] = jnp.where(i == 0, 0.0, halo_ref[...])
    ext_ref[halo:halo + tm, :] = u_ref[...]
    ext_ref[halo + tm:, :] = jnp.zeros((ext_ref.shape[0] - halo - tm, d), F32)
    off = halo - (taps - 1)
    n_sh = sh_ref.shape[1]
    for b in range(8):
        sh_ref[b] = ext_ref[pl.ds(off + b, n_sh), :]

    span = rc + 8 * ((taps - 1) // 8)
    lw = min(d, 256)

    def chunk(r, carry):
        r0 = pl.multiple_of(r * rc, rc)
        for l0 in range(0, d, lw):
            acc = jnp.zeros((rc, lw), F32) + dwb_ref[:, l0:l0 + lw]
            for b in range(8):
                win = sh_ref[b, pl.ds(r0, span), l0:l0 + lw]
                for a in range((taps - 1 - b) // 8 + 1):
                    w = jnp.concatenate([dw8_ref[8 * a + b, :, l0:l0 + lw]] * (rc // 8), axis=0)
                    acc = acc + win[8 * a:8 * a + rc] * w
            c_ref[pl.ds(r0, rc), l0:l0 + lw] = acc
        return carry

    lax.fori_loop(0, tm // rc, chunk, 0)
    o_ref[...] = x_ref[...] + _conv_tail(c_ref[...], lg_ref[...], lb_ref[...], wo_ref[...])


def _conv_sample_body(x_ref, ext_ref, dw_ref, dwb_ref, lg_ref, lb_ref, wo_ref, o_ref, c_ref, *, taps):
    bb, ext_len, d = ext_ref.shape
    s = ext_len - (taps - 1)
    for b in range(bb):
        acc = jnp.zeros((s, d), F32) + dwb_ref[...]
        for k in range(taps):
            acc = acc + ext_ref[b, k:k + s, :] * dw_ref[k:k + 1, :]
        c_ref[b * s:(b + 1) * s, :] = acc
    o_ref[...] = x_ref[...] + _conv_tail(c_ref[...], lg_ref[...], lb_ref[...], wo_ref[...])


def _fox_qkv(x_ref, g_ref, wqkv_ref, wfh_ref, wfl_ref, bf_ref):
    d = x_ref.shape[1]
    hn32 = _rms(x_ref[...], g_ref[...])
    qkv = _dot(hn32.astype(BF16), wqkv_ref[...])
    lg = _dot_hi(hn32, wfh_ref[...], wfl_ref[...]) + bf_ref[...]
    return hn32, qkv[:, :d], qkv[:, d:2 * d], qkv[:, 2 * d:], _log_sigmoid(lg)


def _fox_proj_sample_body(x_ref, g_ref, wqkv_ref, wfh_ref, wfl_ref, bf_ref,
                          q_ref, k_ref, v_ref, lf_ref):
    _, q, k, v, logf = _fox_qkv(x_ref, g_ref, wqkv_ref, wfh_ref, wfl_ref, bf_ref)
    q_ref[...] = q.astype(BF16)
    k_ref[...] = k
    v_ref[...] = v
    lf_ref[...] = logf[:, :lf_ref.shape[1]]


def _fox_proj_prompt_body(x_ref, g_ref, wqkv_ref, wfh_ref, wfl_ref, bf_ref, wfth_ref, wftl_ref, bfc_ref,
                          hsel_ref, q_ref, k_ref, v_ref, kb_ref, vb_ref, lf_ref, f_ref, ft_ref, st_ref,
                          crow_ref, ccol_ref):
    tm = x_ref.shape[0]
    heads = lf_ref.shape[1]
    i = pl.program_id(0)

    @pl.when(i == 0)
    def _():
        crow_ref[...] = jnp.zeros_like(crow_ref)
        ccol_ref[...] = jnp.zeros_like(ccol_ref)

    hn32, q, k, v, logf = _fox_qkv(x_ref, g_ref, wqkv_ref, wfh_ref, wfl_ref, bf_ref)
    q_ref[...] = q.astype(BF16)
    k_ref[...] = k
    v_ref[...] = v
    kb_ref[...] = k.astype(BF16)
    vb_ref[...] = v.astype(BF16)
    lf_ref[...] = logf[:, :heads]

    lane = lax.broadcasted_iota(jnp.int32, (tm, LANES), 1)
    logf = jnp.where(lane < heads, logf, 0.0)
    hh, hl = _split2(hn32)
    lgt = _dot_nt(wfth_ref[...], hh) + _dot_nt(wfth_ref[...], hl) + _dot_nt(wftl_ref[...], hh) + bfc_ref[...]
    sub = lax.broadcasted_iota(jnp.int32, (LANES, tm), 0)
    logft = jnp.where(sub < heads, _log_sigmoid(lgt), 0.0)

    r = lax.broadcasted_iota(jnp.int32, (tm, tm), 0)
    c = lax.broadcasted_iota(jnp.int32, (tm, tm), 1)
    lower = jnp.where(c <= r, 1.0, 0.0).astype(BF16)
    upper = jnp.where(r <= c, 1.0, 0.0).astype(BF16)
    ones_r = jnp.ones((8, tm), BF16)
    ones_c = jnp.ones((tm, LANES), BF16)
    a1, a2, a3 = _split3(logf)
    b1, b2, b3 = _split3(logft)
    f_nat = _dot(lower, a1) + _dot(lower, a2) + _dot(lower, a3) + crow_ref[0:1, :]
    f_ref[...] = f_nat
    ft = _dot(b1, upper) + _dot(b2, upper) + _dot(b3, upper)
    ft_ref[...] = ft + jnp.tile(ccol_ref[...], (1, tm // LANES))
    crow_ref[...] = crow_ref[...] + _dot(ones_r, a1) + _dot(ones_r, a2) + _dot(ones_r, a3)
    ccol_ref[...] = ccol_ref[...] + _dot(b1, ones_c) + _dot(b2, ones_c) + _dot(b3, ones_c)

    def max_head_norm2(z):
        zh, zl = _split2(z * z)
        n2 = _dot(zh, hsel_ref[...]) + _dot(zl, hsel_ref[...])
        return jnp.max(jnp.max(n2, axis=1, keepdims=True), axis=0, keepdims=True)

    sub8 = lax.broadcasted_iota(jnp.int32, (8, LANES), 0)
    st_ref[0] = jnp.where(sub8 == 0, max_head_norm2(q),
                          jnp.where(sub8 == 1, max_head_norm2(k),
                                    jnp.where(sub8 == 2, f_nat[0:1, :],
                                              jnp.where(sub8 == 3, f_nat[tm - 1:tm, :], 0.0))))


def _fox_proj_prompt(x_all, weights, *, t_p, tm, heads):
    d = x_all.shape[1]
    tile = lambda cols: pl.BlockSpec((tm, cols), lambda i: (i, 0))
    sds = lambda cols, dt: jax.ShapeDtypeStruct((t_p, cols), dt)
    n = t_p // tm
    return pl.pallas_call(
        _fox_proj_prompt_body, grid=(n,),
        in_specs=[tile(d)] + [_full_spec(w) for w in weights],
        out_specs=[tile(d)] * 5 + [tile(heads), tile(LANES), pl.BlockSpec((LANES, tm), lambda i: (0, i)),
                                   pl.BlockSpec((1, 8, LANES), lambda i: (i, 0, 0))],
        out_shape=[sds(d, BF16), sds(d, F32), sds(d, F32), sds(d, BF16), sds(d, BF16), sds(heads, F32),
                   sds(LANES, F32), jax.ShapeDtypeStruct((LANES, t_p), F32),
                   jax.ShapeDtypeStruct((n, 8, LANES), F32)],
        scratch_shapes=[pltpu.VMEM((8, LANES), F32), pltpu.VMEM((LANES, LANES), F32)],
        compiler_params=_params(1), name="fox_proj_prompt")(x_all, *weights)


def _fox_attn_prompt_body(start_ref, x_ref, q_ref, k_hbm, v_hbm, f_ref, ft_ref, wo_ref, o_ref,
                          kbuf_ref, vbuf_ref, sem, acc_ref, m_ref, ob_ref, *, heads, hd):
    tq = q_ref.shape[0]
    tk = kbuf_ref.shape[1]
    i = pl.program_id(0)
    j0 = start_ref[i]
    n = i - j0 + 1

    def copies(j, slot):
        rows = pl.ds(pl.multiple_of(j * tk, tk), tk)
        return (pltpu.make_async_copy(k_hbm.at[rows], kbuf_ref.at[slot], sem.at[0, slot]),
                pltpu.make_async_copy(v_hbm.at[rows], vbuf_ref.at[slot], sem.at[1, slot]))

    for cp in copies(j0, 0):
        cp.start()
    acc_ref[...] = jnp.zeros_like(acc_ref)
    m_ref[...] = jnp.full_like(m_ref, NEG)

    lane_k = lax.broadcasted_iota(jnp.int32, (tk, LANES), 1)
    keep_lo = jnp.where(lane_k < hd, 1.0, 0.0).astype(BF16)
    keep_hi = jnp.where(lane_k >= hd, 1.0, 0.0).astype(BF16)
    ones_lo = jnp.where(lane_k == hd, 1.0, 0.0).astype(BF16)
    ones_hi = jnp.where(lane_k == 0, 1.0, 0.0).astype(BF16)

    def block(jj, diagonal):
        j = j0 + jj
        slot = lax.rem(jj, 2)

        @pl.when(jj + 1 < n)
        def _():
            for cp in copies(j + 1, 1 - slot):
                cp.start()

        for cp in copies(j, slot):
            cp.wait()
        if diagonal:
            causal = (lax.broadcasted_iota(jnp.int32, (tq, tk), 1)
                      <= lax.broadcasted_iota(jnp.int32, (tq, tk), 0))
        def scores(h):
            sl = slice((h // 2) * LANES, (h // 2 + 1) * LANES)
            kk = kbuf_ref[slot, :, sl] * (keep_lo if h % 2 == 0 else keep_hi)
            fk = (ft_ref[j, h:h + 1, :] - f_ref[0:1, h:h + 1]) * LOG2E
            t = _dot_nt(q_ref[:, sl], kk) - fk
            return jnp.where(causal, t, NEG) if diagonal else t

        t_next = scores(0)
        for h in range(heads):
            t = t_next
            if h + 1 < heads:
                t_next = scores(h + 1)
            sl = slice((h // 2) * LANES, (h // 2 + 1) * LANES)
            vp = vbuf_ref[slot, :, sl]
            vv = vp * keep_lo + ones_lo if h % 2 == 0 else vp * keep_hi + ones_hi
            m_prev = m_ref[h]
            m_new = jnp.maximum(m_prev, jnp.max(t, axis=1, keepdims=True))
            alpha = jnp.exp2(m_prev - m_new)
            pexp = jnp.exp2(t - jnp.tile(m_new, (1, tk // LANES))).astype(BF16)
            acc_ref[h] = alpha * acc_ref[h] + _dot(pexp, vv)
            m_ref[h] = m_new

    def off_diagonal(jj, carry):
        block(jj, False)
        return carry

    lax.fori_loop(0, n - 1, off_diagonal, 0)
    block(n - 1, True)

    lane_q = lax.broadcasted_iota(jnp.int32, (tq, LANES), 1)
    for p in range(heads // 2):
        ae = acc_ref[2 * p]
        ao = acc_ref[2 * p + 1]
        o = jnp.where(lane_q < hd, ae / ae[:, hd:hd + 1], ao / ao[:, 0:1])
        ob_ref[:, p * LANES:(p + 1) * LANES] = o.astype(BF16)
    o_ref[...] = x_ref[...] + _dot(ob_ref[...], wo_ref[...])


SKIP_MARGIN = 100.0


def _fox_band_starts(stats, heads):
    nq = stats.shape[0]
    qmax = jnp.sqrt(stats[:, 0, 0]) * (1.01 / LOG2E)
    kmax = jnp.sqrt(stats[:, 1, 0]) * 1.01
    f_first = stats[:, 2, :heads]
    f_last = stats[:, 3, :heads]
    gap = jnp.max(f_first[:, None, :] - f_last[None, :, :], axis=2)
    bound = gap + qmax[:, None] * (kmax[None, :] + kmax[:, None]) + 0.01
    tile = jnp.arange(nq, dtype=jnp.int32)
    needed = jnp.logical_or(bound >= -SKIP_MARGIN, tile[None, :] == tile[:, None])
    needed = jnp.logical_and(needed, tile[None, :] <= tile[:, None])
    return jnp.argmax(needed, axis=1).astype(jnp.int32)


def _fox_attn_prompt(x_all, q, kb, vb, f_nat, f_t, stats, w_out, *, t_p, tq, heads):
    d = x_all.shape[1]
    hd = d // heads
    nq = t_p // tq
    starts = _fox_band_starts(stats, heads)
    ft3 = jnp.swapaxes(f_t[:heads].reshape(heads, nq, tq), 0, 1)
    grid_spec = pltpu.PrefetchScalarGridSpec(
        num_scalar_prefetch=1, grid=(nq,),
        in_specs=[
            pl.BlockSpec((tq, d), lambda i, st: (i, 0)),
            pl.BlockSpec((tq, d), lambda i, st: (i, 0)),
            pl.BlockSpec(memory_space=pl.ANY),
            pl.BlockSpec(memory_space=pl.ANY),
            pl.BlockSpec((tq, LANES), lambda i, st: (i, 0)),
            pl.BlockSpec((nq, heads, tq), lambda i, st: (0, 0, 0)),
            pl.BlockSpec((d, d), lambda i, st: (0, 0)),
        ],
        out_specs=pl.BlockSpec((tq, d), lambda i, st: (i, 0)),
        scratch_shapes=[pltpu.VMEM((2, tq, d), BF16), pltpu.VMEM((2, tq, d), BF16),
                        pltpu.SemaphoreType.DMA((2, 2)),
                        pltpu.VMEM((heads, tq, LANES), F32), pltpu.VMEM((heads, tq, LANES), F32),
                        pltpu.VMEM((tq, d), BF16)])
    return pl.pallas_call(
        functools.partial(_fox_attn_prompt_body, heads=heads, hd=hd),
        grid_spec=grid_spec, out_shape=jax.ShapeDtypeStruct(x_all.shape, F32),
        input_output_aliases={1: 0}, compiler_params=_params(1), name="fox_attn_prompt",
    )(starts, x_all, q, kb, vb, f_nat, ft3, w_out)


def _fox_attn_sample_body(q_ref, kn_ref, vn_ref, ck_ref, cv_ref, lft_ref, o_ref,
                          qbd_ref, bias_ref, tail_ref, m_ref, l_ref, acc_ref, *, heads, n_chunks):
    s_new, d = q_ref.shape
    hd = d // heads
    rows = heads * s_new
    kc_len = ck_ref.shape[1]
    past = kc_len * n_chunks
    width = lft_ref.shape[2]
    kc = pl.program_id(1)
    row_h = lax.broadcasted_iota(jnp.int32, (rows, d), 0) // s_new
    head_mask = row_h == lax.broadcasted_iota(jnp.int32, (rows, d), 1) // hd

    @pl.when(kc == 0)
    def _():
        qt = jnp.concatenate([q_ref[...].astype(F32)] * heads, axis=0)
        qbd_ref[...] = jnp.where(head_mask, qt, 0.0).astype(BF16)
        lf = lft_ref[0]
        lane = lax.broadcasted_iota(jnp.int32, (heads, width), 1)
        run = lf
        shift = 1
        while shift < width:
            moved = pltpu.roll(run, width - shift, 1)
            run = run + jnp.where(lane + shift < width, moved, 0.0)
            shift *= 2
        rexcl = run - lf
        rexp = jnp.concatenate(
            [jnp.broadcast_to(rexcl[h:h + 1, :], (s_new, width)) for h in range(heads)], axis=0)
        tail = rexp[:, past:]
        tl = lax.broadcasted_iota(jnp.int32, tail.shape, 1)
        tr = lax.broadcasted_iota(jnp.int32, tail.shape, 0) % s_new
        rq = jnp.sum(jnp.where(tl == tr, tail, 0.0), axis=1, keepdims=True)
        for c in range(n_chunks):
            bias_ref[c] = rexp[:, c * kc_len:(c + 1) * kc_len] - rq
        tail_ref[...] = tail - rq
        m_ref[...] = jnp.full_like(m_ref, NEG)
        l_ref[...] = jnp.zeros_like(l_ref)
        acc_ref[...] = jnp.zeros_like(acc_ref)

    def update(s, vmat):
        m_prev = m_ref[...]
        m_new = jnp.maximum(m_prev, jnp.max(s, axis=1, keepdims=True))
        alpha = jnp.exp(m_prev - m_new)
        p = jnp.exp(s - m_new)
        l_ref[...] = alpha * l_ref[...] + jnp.sum(p, axis=1, keepdims=True)
        acc_ref[...] = alpha * acc_ref[...] + _dot(p.astype(BF16), vmat)
        m_ref[...] = m_new

    update(_dot_nt(qbd_ref[...], ck_ref[0]) + bias_ref[kc], cv_ref[0])

    @pl.when(kc == n_chunks - 1)
    def _():
        pad = jnp.zeros((width - past - s_new, d), BF16)
        kn = jnp.concatenate([kn_ref[...].astype(BF16), pad], axis=0)
        vn = jnp.concatenate([vn_ref[...].astype(BF16), pad], axis=0)
        s = _dot_nt(qbd_ref[...], kn) + tail_ref[...]
        tl = lax.broadcasted_iota(jnp.int32, s.shape, 1)
        tr = lax.broadcasted_iota(jnp.int32, s.shape, 0) % s_new
        update(jnp.where(tl <= tr, s, NEG), vn)
        o_full = jnp.where(head_mask, acc_ref[...] / l_ref[...], 0.0)
        out = o_full[0:s_new, :]
        for h in range(1, heads):
            out = out + o_full[h * s_new:(h + 1) * s_new, :]
        o_ref[...] = out.astype(BF16)


def _fox_attn_sample(q_s, k_s, v_s, cache_k, cache_v, lft_all, *, heads, s_new):
    t_s, d = q_s.shape
    n_b = t_s // s_new
    past = cache_k.shape[1]
    n_chunks = 1
    kc_len = past // n_chunks
    width = lft_all.shape[2]
    rows = heads * s_new
    return pl.pallas_call(
        functools.partial(_fox_attn_sample_body, heads=heads, n_chunks=n_chunks),
        grid=(n_b, n_chunks),
        in_specs=[
            pl.BlockSpec((s_new, d), lambda b, c: (b, 0)),
            pl.BlockSpec((s_new, d), lambda b, c: (b, 0)),
            pl.BlockSpec((s_new, d), lambda b, c: (b, 0)),
            pl.BlockSpec((1, kc_len, d), lambda b, c: (b, c, 0)),
            pl.BlockSpec((1, kc_len, d), lambda b, c: (b, c, 0)),
            pl.BlockSpec((1, heads, width), lambda b, c: (b, 0, 0)),
        ],
        out_specs=pl.BlockSpec((s_new, d), lambda b, c: (b, 0)),
        out_shape=jax.ShapeDtypeStruct((t_s, d), BF16),
        scratch_shapes=[pltpu.VMEM((rows, d), BF16), pltpu.VMEM((n_chunks, rows, kc_len), F32),
                        pltpu.VMEM((rows, width - past), F32),
                        pltpu.VMEM((rows, 1), F32), pltpu.VMEM((rows, 1), F32),
                        pltpu.VMEM((rows, d), F32)],
        compiler_params=_params(2), name="fox_attn_sample")(q_s, k_s, v_s, cache_k, cache_v, lft_all)


def _proj_residual_body(x_ref, o_ref_in, w_ref, o_ref):
    o_ref[...] = x_ref[...] + _dot(o_ref_in[...], w_ref[...])


def _sgu_body(x_ref, g_ref, win_ref, lg_ref, lb_ref, ws_ref, bs_ref, wo_ref, o_ref, v_ref, mix_ref,
              *, n_prompt_tiles, groups):
    tm, d = x_ref.shape
    gc = d // groups
    i = pl.program_id(0)
    sel = (i >= n_prompt_tiles).astype(jnp.int32)
    hn = _rms(x_ref[...], g_ref[...]).astype(BF16)
    z = _dot(hn, win_ref[...])
    z = 0.5 * z * (1.0 + lax.erf(z * (2.0 ** -0.5)))
    u = z[:, :d]
    v = _ln(z[:, d:], lg_ref[...], lb_ref[...])
    v_ref[...] = v
    vb = v.astype(BF16)
    for c in range(tm // SGU_CHUNK):
        rs = slice(c * SGU_CHUNK, (c + 1) * SGU_CHUNK)
        for g in range(groups):
            cs = slice(g * gc, (g + 1) * gc)
            mix_ref[rs, cs] = _dot(ws_ref[sel, g], vb[rs, cs])
        mix_ref[rs, :] = mix_ref[rs, :] + bs_ref[sel]
    o_ref[...] = x_ref[...] + _dot((u * mix_ref[...]).astype(BF16), wo_ref[...])


def _xattn_core(q, k_of, v_of, heads):
    d = q.shape[1]
    hd = d // heads
    outs = []
    for h in range(heads):
        s = _dot_nt(q[:, h * hd:(h + 1) * hd], k_of(h))
        p = jnp.exp(s - jnp.max(s, axis=1, keepdims=True))
        p = p / jnp.sum(p, axis=1, keepdims=True)
        outs.append(_dot(p.astype(BF16), v_of(h)).astype(BF16))
    return outs


def _xattn_prompt_body(x_ref, g_ref, wq_ref, k_ref, v_ref, wo_ref, o_ref, *, heads, layer):
    d = x_ref.shape[1]
    hd = d // heads
    x = x_ref[...]
    q = _dot(_rms(x, g_ref[...]).astype(BF16), wq_ref[...]).astype(BF16)
    k = k_ref[layer].astype(BF16)
    v = v_ref[layer].astype(BF16)
    outs = _xattn_core(q, lambda h: k[:, h * hd:(h + 1) * hd], lambda h: v[:, h * hd:(h + 1) * hd], heads)
    y = x
    for h, oh in enumerate(outs):
        y = y + _dot(oh, wo_ref[h * hd:(h + 1) * hd, :])
    o_ref[...] = y


def _xattn_sample_body(x_ref, g_ref, wq_ref, k_ref, v_ref, wo_ref, o_ref, ob_ref, *, heads, s_new):
    d = x_ref.shape[1]
    hd = d // heads
    bb = k_ref.shape[1]
    x = x_ref[...]
    q = _dot(_rms(x, g_ref[...]).astype(BF16), wq_ref[...]).astype(BF16)
    for b in range(bb):
        rs = slice(b * s_new, (b + 1) * s_new)
        outs = _xattn_core(q[rs, :], lambda h, b=b: k_ref[0, b, :, h, :].astype(BF16),
                           lambda h, b=b: v_ref[0, b, :, h, :].astype(BF16), heads)
        for h, oh in enumerate(outs):
            ob_ref[rs, h * hd:(h + 1) * hd] = oh
    o_ref[...] = x + _dot(ob_ref[...], wo_ref[...])


def _ffn_body(x_ref, g_ref, wg_ref, wu_ref, wd_ref, o_ref, hn_ref, acc_ref):
    f = pl.program_id(1)

    @pl.when(f == 0)
    def _():
        hn_ref[...] = _rms(x_ref[...], g_ref[...]).astype(BF16)
        acc_ref[...] = jnp.zeros_like(acc_ref)

    hn = hn_ref[...]
    a = _silu(_dot(hn, wg_ref[...])) * _dot(hn, wu_ref[...])
    acc_ref[...] += _dot(a.astype(BF16), wd_ref[...])

    @pl.when(f == pl.num_programs(1) - 1)
    def _():
        o_ref[...] = x_ref[...] + acc_ref[...]


def _ffn(x_all, g, w_gu, w_down, *, tm, tf):
    m, d = x_all.shape
    ff = w_down.shape[0]
    nf = ff // tf
    return pl.pallas_call(
        _ffn_body, grid=(m // tm, nf),
        in_specs=[pl.BlockSpec((tm, d), lambda i, f: (i, 0)),
                  _full_spec(g),
                  pl.BlockSpec((d, tf), lambda i, f: (0, f)),
                  pl.BlockSpec((d, tf), lambda i, f, nf=nf: (0, nf + f)),
                  pl.BlockSpec((tf, d), lambda i, f: (f, 0))],
        out_specs=pl.BlockSpec((tm, d), lambda i, f: (i, 0)),
        out_shape=jax.ShapeDtypeStruct((m, d), F32),
        input_output_aliases={0: 0},
        scratch_shapes=[pltpu.VMEM((tm, d), BF16), pltpu.VMEM((tm, d), F32)],
        compiler_params=_params(2), name="ffn")(x_all, g, w_gu, w_gu, w_down)


def _router_body(x_ref, g_ref, rh_ref, rl_ref, idx_ref, gate_ref, rank_ref, cnt_ref, carry_ref, *, n_experts):
    tm = x_ref.shape[0]

    @pl.when(pl.program_id(0) == 0)
    def _():
        carry_ref[...] = jnp.zeros_like(carry_ref)

    hn32 = _rms(x_ref[...], g_ref[...])
    logits = _dot_hi(hn32, rh_ref[...], rl_ref[...])
    lane = lax.broadcasted_iota(jnp.int32, logits.shape, 1)
    lane_f = lane.astype(F32)
    big = float(LANES)
    masked = jnp.where(lane < n_experts, logits, -jnp.inf)
    m1 = jnp.max(masked, axis=1, keepdims=True)
    i1 = jnp.min(jnp.where(masked == m1, lane_f, big), axis=1, keepdims=True)
    masked2 = jnp.where(lane_f == i1, -jnp.inf, masked)
    m2 = jnp.max(masked2, axis=1, keepdims=True)
    i2 = jnp.min(jnp.where(masked2 == m2, lane_f, big), axis=1, keepdims=True)
    e = jnp.exp(m2 - m1)
    w1 = 1.0 / (1.0 + e)
    w2 = e * w1
    idx_ref[...] = jnp.where(lane == 0, i1, jnp.where(lane == 1, i2, 0.0)).astype(jnp.int32)
    gate_ref[...] = jnp.where(lane == 0, w1, jnp.where(lane == 1, w2, 0.0))
    hit1 = lane_f == i1
    hit2 = lane_f == i2
    onehot = jnp.where(jnp.logical_or(hit1, hit2), 1.0, 0.0).astype(BF16)
    r = lax.broadcasted_iota(jnp.int32, (tm, tm), 0)
    c = lax.broadcasted_iota(jnp.int32, (tm, tm), 1)
    before = jnp.where(c < r, 1.0, 0.0).astype(BF16)
    base = _dot(before, onehot) + carry_ref[0:1, :]
    r1 = jnp.sum(jnp.where(hit1, base, 0.0), axis=1, keepdims=True)
    r2 = jnp.sum(jnp.where(hit2, base, 0.0), axis=1, keepdims=True)
    rank_ref[...] = jnp.where(lane == 0, r1, jnp.where(lane == 1, r2, 0.0)).astype(jnp.int32)
    carry_ref[...] = carry_ref[...] + _dot(jnp.ones((8, tm), BF16), onehot)
    cnt_ref[...] = carry_ref[...]


def _moe_router(x_all, g, r_hi, r_lo, *, tm, n_experts):
    m, d = x_all.shape
    tile = pl.BlockSpec((tm, LANES), lambda i: (i, 0))
    return pl.pallas_call(
        functools.partial(_router_body, n_experts=n_experts), grid=(m // tm,),
        in_specs=[pl.BlockSpec((tm, d), lambda i: (i, 0)), _full_spec(g), _full_spec(r_hi), _full_spec(r_lo)],
        out_specs=[tile, tile, tile, pl.BlockSpec((8, LANES), lambda i: (0, 0))],
        out_shape=[jax.ShapeDtypeStruct((m, LANES), jnp.int32), jax.ShapeDtypeStruct((m, LANES), F32),
                   jax.ShapeDtypeStruct((m, LANES), jnp.int32), jax.ShapeDtypeStruct((8, LANES), F32)],
        scratch_shapes=[pltpu.VMEM((8, LANES), F32)],
        compiler_params=_params(1), name="moe_router")(x_all, g, r_hi, r_lo)


def _row_copies(src_ref, dst_ref, idx_ref, base, sem, n, *, gather, stride=1, offset=0):
    def copy(j):
        row = idx_ref[base + j * stride + offset]
        if gather:
            return pltpu.make_async_copy(src_ref.at[pl.ds(row, 1)], dst_ref.at[pl.ds(j, 1)], sem)
        return pltpu.make_async_copy(src_ref.at[pl.ds(j, 1)], dst_ref.at[pl.ds(row, 1)], sem)

    def wait(j, c):
        copy(j).wait()
        return c

    for j in range(n):
        copy(j).start(priority=j % 2)
    lax.fori_loop(0, n, wait, 0, unroll=8)


def _moe_dispatch_body(dest_ref, tend_ref, x_ref, g_ref, out_hbm, hbuf_ref, sem0, sem1):
    tm = x_ref.shape[0]
    base = pl.program_id(0) * (tm * TOP_K)

    @pl.when(pl.program_id(0) == 0)
    def _():
        hbuf_ref[...] = jnp.zeros_like(hbuf_ref)
        n_experts = tend_ref.shape[0]
        n_used = tend_ref[n_experts - 1]
        r_max = out_hbm.shape[0] // tm

        def fill(tile):
            cp = pltpu.make_async_copy(hbuf_ref, out_hbm.at[pl.ds(pl.multiple_of(tile * tm, tm), tm)], sem0)
            cp.start()
            cp.wait()

        for e in range(n_experts):
            fill(jnp.maximum(tend_ref[e] - 1, 0))
        for k in range(n_experts - 1):
            @pl.when(n_used + k < r_max)
            def _(k=k):
                fill(n_used + k)

    hbuf_ref[...] = _rms(x_ref[...], g_ref[...])
    _row_copies(hbuf_ref, out_hbm, dest_ref, base, sem0, tm, gather=False, stride=TOP_K, offset=0)
    _row_copies(hbuf_ref, out_hbm, dest_ref, base, sem1, tm, gather=False, stride=TOP_K, offset=1)


def _moe_dispatch(x_all, g, dest, tile_end, *, tm, rows_sorted):
    m, d = x_all.shape
    grid_spec = pltpu.PrefetchScalarGridSpec(
        num_scalar_prefetch=2, grid=(m // tm,),
        in_specs=[pl.BlockSpec((tm, d), lambda i, dest, tend: (i, 0)),
                  pl.BlockSpec((1, d), lambda i, dest, tend: (0, 0))],
        out_specs=pl.BlockSpec(memory_space=pl.ANY),
        scratch_shapes=[pltpu.VMEM((tm, d), F32), pltpu.SemaphoreType.DMA(()), pltpu.SemaphoreType.DMA(())])
    return pl.pallas_call(
        _moe_dispatch_body, grid_spec=grid_spec, out_shape=jax.ShapeDtypeStruct((rows_sorted, d), F32),
        compiler_params=_params(1), name="moe_dispatch")(dest, tile_end, x_all, g)


def _moe_ffn_body(te_ref, nt_ref, hs_ref, wg_ref, wu_ref, wd_ref, o_ref, hb_ref, acc_ref):
    r = pl.program_id(0)
    f = pl.program_id(1)
    valid = r < nt_ref[0]

    @pl.when(jnp.logical_and(valid, f == 0))
    def _():
        hb_ref[...] = hs_ref[...].astype(BF16)
        acc_ref[...] = jnp.zeros_like(acc_ref)

    @pl.when(valid)
    def _():
        h = hb_ref[...]
        a = _silu(_dot(h, wg_ref[0, 0])) * _dot(h, wu_ref[0, 0])
        acc_ref[...] += _dot(a.astype(BF16), wd_ref[0, 0])

    last = f == pl.num_programs(1) - 1

    @pl.when(jnp.logical_and(valid, last))
    def _():
        o_ref[...] = acc_ref[...]

    @pl.when(jnp.logical_and(jnp.logical_not(valid), last))
    def _():
        o_ref[...] = jnp.zeros_like(o_ref)


def _moe_ffn(hn_sorted, tile_expert, n_tiles, w_gu, w_down, *, layer, tm, tf):
    rows, d = hn_sorted.shape
    d_exp = w_down.shape[2]
    nf = d_exp // tf

    def fe(r, f, nt):
        return jnp.where(r < nt[0], f, nf - 1)

    grid_spec = pltpu.PrefetchScalarGridSpec(
        num_scalar_prefetch=2, grid=(rows // tm, nf),
        in_specs=[
            pl.BlockSpec((tm, d), lambda r, f, te, nt: (jnp.minimum(r, nt[0] - 1), 0)),
            pl.BlockSpec((1, 1, d, tf), lambda r, f, te, nt: (layer, te[r], 0, fe(r, f, nt))),
            pl.BlockSpec((1, 1, d, tf), lambda r, f, te, nt: (layer, te[r], 0, nf + fe(r, f, nt))),
            pl.BlockSpec((1, 1, tf, d), lambda r, f, te, nt: (layer, te[r], fe(r, f, nt), 0)),
        ],
        out_specs=pl.BlockSpec((tm, d), lambda r, f, te, nt: (r, 0)),
        scratch_shapes=[pltpu.VMEM((tm, d), BF16), pltpu.VMEM((tm, d), F32)])
    return pl.pallas_call(
        _moe_ffn_body, grid_spec=grid_spec, out_shape=jax.ShapeDtypeStruct((rows, d), F32),
        compiler_params=_params(2), name="moe_ffn")(tile_expert, n_tiles, hn_sorted, w_gu, w_gu, w_down)


def _moe_combine_nonorm_body(dest_ref, x_ref, gate_ref, y_hbm, o_ref, b0_ref, b1_ref, sem0, sem1):
    _moe_combine_body(dest_ref, x_ref, gate_ref, y_hbm, None, o_ref, b0_ref, b1_ref, sem0, sem1)


def _moe_combine_body(dest_ref, x_ref, gate_ref, y_hbm, gn_ref, o_ref, b0_ref, b1_ref, sem0, sem1):
    tm = x_ref.shape[0]
    base = pl.program_id(0) * (tm * TOP_K)
    _row_copies(y_hbm, b0_ref, dest_ref, base, sem0, tm, gather=True, stride=TOP_K, offset=0)
    _row_copies(y_hbm, b1_ref, dest_ref, base, sem1, tm, gather=True, stride=TOP_K, offset=1)
    gate = gate_ref[...]
    y = x_ref[...] + gate[:, 0:1] * b0_ref[...] + gate[:, 1:2] * b1_ref[...]
    o_ref[...] = y if gn_ref is None else _rms(y, gn_ref[...])


def _moe_combine(x_all, gates, dest, y_sorted, *, tm, final_gain=None):
    m, d = x_all.shape
    body = _moe_combine_body
    extra_specs, extra_args = [], []
    if final_gain is None:
        body = _moe_combine_nonorm_body
    else:
        extra_specs, extra_args = [pl.BlockSpec((1, d), lambda i, dest: (0, 0))], [final_gain]
    grid_spec = pltpu.PrefetchScalarGridSpec(
        num_scalar_prefetch=1, grid=(m // tm,),
        in_specs=[pl.BlockSpec((tm, d), lambda i, dest: (i, 0)),
                  pl.BlockSpec((tm, LANES), lambda i, dest: (i, 0)),
                  pl.BlockSpec(memory_space=pl.ANY)] + extra_specs,
        out_specs=pl.BlockSpec((tm, d), lambda i, dest: (i, 0)),
        scratch_shapes=[pltpu.VMEM((tm, d), F32), pltpu.VMEM((tm, d), F32),
                        pltpu.SemaphoreType.DMA(()), pltpu.SemaphoreType.DMA(())])
    return pl.pallas_call(
        body, grid_spec=grid_spec, out_shape=jax.ShapeDtypeStruct((m, d), F32),
        input_output_aliases={1: 0}, compiler_params=_params(1), name="moe_combine",
    )(dest, x_all, gates, y_sorted, *extra_args)


def _moe_plan(top_i, rank, counts, tm, r_max):
    n_experts = counts.shape[0]
    tiles = (counts + tm - 1) // tm
    tile_end = jnp.cumsum(tiles)
    row_start = (tile_end - tiles) * tm
    onehot = top_i[:, :, None] == jnp.arange(n_experts, dtype=jnp.int32)[None, None, :]
    dest = (jnp.sum(jnp.where(onehot, row_start[None, None, :], 0), axis=2) + rank).astype(jnp.int32)
    n_tiles = tile_end[-1:].astype(jnp.int32)
    tile_id = jnp.arange(r_max, dtype=jnp.int32)
    tile_expert = jnp.minimum(jnp.sum((tile_id[:, None] >= tile_end[None, :]).astype(jnp.int32), axis=1),
                              n_experts - 1).astype(jnp.int32)
    return dest.reshape(-1), n_tiles, tile_expert, tile_end.astype(jnp.int32)


def _final_norm_body(x_ref, g_ref, o_ref):
    o_ref[...] = _rms(x_ref[...], g_ref[...])


def _pick(n, cands):
    for c in cands:
        if n % c == 0:
            return c
    raise ValueError(f"no tile in {cands} divides {n}")


def kernel(x_prompt, x_sample, mem_prompt, cache_mem_k, cache_mem_v, state_conv, cache_fox_k, cache_fox_v, cache_fox_logf, norm_mix, norm_mem_q, norm_ffn, norm_mem, norm_final, conv_w_in, conv_dw, conv_dw_b, conv_ln_g, conv_ln_b, conv_w_out, fox_w_in, fox_b_f, fox_w_out, sgu_w_in, sgu_ln_g, sgu_ln_b, sgu_w_s, sgu_b_s, sgu_w_out, xa_w_q, xa_w_kv, xa_w_o, ffn_w_gu, ffn_w_down, moe_router, moe_w_gu, moe_w_down):
    batch, t_p, d = x_prompt.shape
    n_b, s_new, _ = x_sample.shape
    assert batch == 1
    t_s = n_b * s_new
    m = t_p + t_s
    depth = norm_mix.shape[0]
    n_mem = mem_prompt.shape[1]
    xa_heads = cache_mem_k.shape[3]
    taps = conv_dw.shape[1]
    fox_heads = fox_b_f.shape[1]
    past = cache_fox_k.shape[2]
    groups = sgu_w_s.shape[1]
    n_experts = moe_router.shape[2]
    d_exp = moe_w_down.shape[2]
    d_ff = ffn_w_down.shape[1]

    tm = _pick(int(np.gcd(t_p, t_s)), (512, 256, 128))
    n_pt = t_p // tm
    n_st = t_s // tm
    n_t = n_pt + n_st
    halo = 32
    assert taps - 1 <= halo and tm % halo == 0 and tm % SGU_CHUNK == 0 and SGU_CHUNK % s_new == 0
    assert fox_heads % 2 == 0 and d // fox_heads * 2 == LANES and fox_heads <= LANES
    assert past % SGU_CHUNK == 0 and s_new <= SGU_BLOCK

    row = lambda a: a.reshape(1, -1)
    x_all = jnp.concatenate([x_prompt[0], x_sample.reshape(t_s, d)], axis=0)

    mem_k, mem_v = _mem_kv(mem_prompt[0], row(norm_mem), xa_w_kv.astype(BF16))
    moe_w_gu_b = moe_w_gu.astype(BF16)
    moe_w_down_b = moe_w_down.astype(BF16)

    conv_p, conv_s = [], []
    fox_out = []
    sgu_v = []
    for i in range(depth):
        kind = i % 3
        j = i // 3
        g_mix = row(norm_mix[i])
        if kind == 0:
            (u_all,) = _row_call(
                _conv_in_body, grid_tiles=n_t, tm=tm, first_tile=0, row_ins=[x_all],
                full_ins=[g_mix, conv_w_in[j].astype(BF16)], outs=[(m, d, F32, 0)], alias_x=False,
                name="conv_in")
            dw = jnp.concatenate([conv_dw[j], jnp.zeros((halo - taps, d), F32)], axis=0)
            tail_w = [dw, row(conv_dw_b[j]), row(conv_ln_g[j]), row(conv_ln_b[j]), conv_w_out[j].astype(BF16)]
            hb = tm // halo
            prompt_w = [jnp.broadcast_to(conv_dw[j][:, None, :], (taps, 8, d))] + tail_w[1:]
            x_all = pl.pallas_call(
                functools.partial(_conv_prompt_body, taps=taps, halo=halo, rc=32), grid=(n_pt,),
                in_specs=[pl.BlockSpec((tm, d), lambda t: (t, 0)),
                          pl.BlockSpec((tm, d), lambda t: (t, 0)),
                          pl.BlockSpec((halo, d), lambda t, hb=hb: (jnp.maximum(t * hb - 1, 0), 0))]
                         + [_full_spec(a) for a in prompt_w],
                out_specs=pl.BlockSpec((tm, d), lambda t: (t, 0)),
                out_shape=jax.ShapeDtypeStruct((m, d), F32), input_output_aliases={0: 0},
                scratch_shapes=[pltpu.VMEM((tm + halo + 8, d), F32),
                                pltpu.VMEM((8, tm + 8 * ((taps - 1) // 8), d), F32),
                                pltpu.VMEM((tm, d), F32)],
                compiler_params=_params(1), name="conv_prompt")(x_all, u_all, u_all, *prompt_w)
            u_s = u_all[t_p:].reshape(n_b, s_new, d)
            ext_s = jnp.concatenate([state_conv[j], u_s], axis=1)
            bb = _pick(n_b, (8, 4, 2, 1))
            rows_s = bb * s_new
            x_all = pl.pallas_call(
                functools.partial(_conv_sample_body, taps=taps), grid=(n_b // bb,),
                in_specs=[pl.BlockSpec((rows_s, d), lambda t, o=t_p // rows_s: (t + o, 0)),
                          pl.BlockSpec((bb, taps - 1 + s_new, d), lambda t: (t, 0, 0))]
                         + [_full_spec(a) for a in tail_w],
                out_specs=pl.BlockSpec((rows_s, d), lambda t, o=t_p // rows_s: (t + o, 0)),
                out_shape=jax.ShapeDtypeStruct((m, d), F32), input_output_aliases={0: 0},
                scratch_shapes=[pltpu.VMEM((rows_s, d), F32)],
                compiler_params=_params(1), name="conv_sample")(x_all, ext_s, *tail_w)
            conv_p.append(u_all[t_p - (taps - 1):t_p][None])
            conv_s.append(ext_s[:, s_new:])
        elif kind == 1:
            hd = d // fox_heads
            w_in = fox_w_in[j]
            w_qkv = jnp.concatenate([w_in[:, :d] * (hd ** -0.5), w_in[:, d:3 * d]], axis=1).astype(BF16)
            w_qkv_p = jnp.concatenate([w_in[:, :d] * (hd ** -0.5 * LOG2E), w_in[:, d:3 * d]],
                                      axis=1).astype(BF16)
            w_f = jnp.concatenate([w_in[:, 3 * d:], jnp.zeros((d, LANES - fox_heads), F32)], axis=1)
            wf_hi = w_f.astype(BF16)
            wf_lo = (w_f - wf_hi.astype(F32)).astype(BF16)
            b_f = jnp.concatenate([fox_b_f[j], jnp.zeros((LANES - fox_heads,), F32)])
            b_row = row(b_f)
            ta = min(tm, FOX_TILE)
            b_col = jnp.broadcast_to(b_f[:, None], (LANES, ta))
            head_sel = (jnp.arange(d)[:, None] // hd == jnp.arange(LANES)[None, :]).astype(BF16)
            proj_w = [g_mix, w_qkv, wf_hi, wf_lo, b_row]
            q_p, k_p, v_p, kb_p, vb_p, lf_p, f_nat, f_t, stats = _fox_proj_prompt(
                x_all, [g_mix, w_qkv_p, wf_hi, wf_lo, b_row, wf_hi.T, wf_lo.T, b_col, head_sel],
                t_p=t_p, tm=ta, heads=fox_heads)
            w_out = fox_w_out[j].astype(BF16)
            x_all = _fox_attn_prompt(x_all, q_p, kb_p, vb_p, f_nat, f_t, stats, w_out, t_p=t_p, tq=ta,
                                     heads=fox_heads)
            q_s, k_s, v_s, lf_s = _row_call(
                _fox_proj_sample_body, grid_tiles=n_st, tm=tm, first_tile=n_pt, row_ins=[x_all],
                full_ins=proj_w,
                outs=[(t_s, d, BF16, 0), (t_s, d, F32, 0), (t_s, d, F32, 0), (t_s, fox_heads, F32, 0)],
                alias_x=False, name="fox_proj_sample")
            width = past + LANES
            lft_all = jnp.concatenate(
                [jnp.swapaxes(cache_fox_logf[j], 1, 2),
                 jnp.swapaxes(lf_s.reshape(n_b, s_new, fox_heads), 1, 2),
                 jnp.zeros((n_b, fox_heads, width - past - s_new), F32)], axis=2)
            o_s = _fox_attn_sample(q_s, k_s, v_s, cache_fox_k[j].astype(BF16).reshape(n_b, past, d),
                                   cache_fox_v[j].astype(BF16).reshape(n_b, past, d), lft_all,
                                   heads=fox_heads, s_new=s_new)
            x_all = pl.pallas_call(
                _proj_residual_body, grid=(n_st,),
                in_specs=[pl.BlockSpec((tm, d), lambda t, o=n_pt: (t + o, 0)),
                          pl.BlockSpec((tm, d), lambda t: (t, 0)), _full_spec(w_out)],
                out_specs=pl.BlockSpec((tm, d), lambda t, o=n_pt: (t + o, 0)),
                out_shape=jax.ShapeDtypeStruct((m, d), F32), input_output_aliases={0: 0},
                compiler_params=_params(1), name="fox_out_sample")(x_all, o_s, w_out)
            fox_out.append((k_p, v_p, lf_p, k_s, v_s, lf_s))
        else:
            ws = sgu_w_s[j]
            pos = jnp.arange(SGU_CHUNK)
            blk_mask = (pos[None, :] // SGU_BLOCK) <= (pos[:, None] // SGU_BLOCK)
            ws_p = jnp.where(blk_mask[None], ws, 0.0)
            reps = SGU_CHUNK // s_new
            eye = jnp.eye(reps, dtype=F32)
            ws_s = jnp.einsum("ab,gij->gaibj", eye, ws_p[:, :s_new, :s_new]).reshape(
                groups, SGU_CHUNK, SGU_CHUNK)
            ws_all = jnp.stack([ws_p, ws_s]).astype(BF16)
            gc = d // groups
            bs_p = jnp.repeat(sgu_b_s[j].T, gc, axis=1)
            bs_s = jnp.tile(bs_p[:s_new], (reps, 1))
            bs_all = jnp.stack([bs_p, bs_s])
            x_all, v_all = _row_call(
                functools.partial(_sgu_body, n_prompt_tiles=n_pt, groups=groups),
                grid_tiles=n_t, tm=tm, first_tile=0, row_ins=[x_all],
                full_ins=[g_mix, sgu_w_in[j].astype(BF16), row(sgu_ln_g[j]), row(sgu_ln_b[j]),
                          ws_all, bs_all, sgu_w_out[j].astype(BF16)],
                outs=[(m, d, F32, 0), (m, d, F32, 0)], scratch=[pltpu.VMEM((tm, d), F32)], name="sgu")
            sgu_v.append(v_all[t_p:].reshape(n_b, s_new, d))

        xa_hd = d // xa_heads
        wq = (xa_w_q[i] * (xa_hd ** -0.5)).astype(BF16)
        wo = xa_w_o[i].astype(BF16)
        g_q = row(norm_mem_q[i])
        (x_all,) = _row_call(
            functools.partial(_xattn_prompt_body, heads=xa_heads, layer=i),
            grid_tiles=n_pt, tm=tm, first_tile=0, row_ins=[x_all],
            full_ins=[g_q, wq, mem_k, mem_v, wo], outs=[(m, d, F32, 0)], name="xattn_prompt")
        bb = _pick(n_b, (4, 2, 1))
        rows_s = bb * s_new
        x_all = pl.pallas_call(
            functools.partial(_xattn_sample_body, heads=xa_heads, s_new=s_new), grid=(n_b // bb,),
            in_specs=[pl.BlockSpec((rows_s, d), lambda t, o=t_p // rows_s: (t + o, 0)),
                      _full_spec(g_q), _full_spec(wq),
                      pl.BlockSpec((1, bb, n_mem, xa_heads, xa_hd), lambda t, i=i: (i, t, 0, 0, 0)),
                      pl.BlockSpec((1, bb, n_mem, xa_heads, xa_hd), lambda t, i=i: (i, t, 0, 0, 0)),
                      _full_spec(wo)],
            out_specs=pl.BlockSpec((rows_s, d), lambda t, o=t_p // rows_s: (t + o, 0)),
            out_shape=jax.ShapeDtypeStruct((m, d), F32), input_output_aliases={0: 0},
            scratch_shapes=[pltpu.VMEM((rows_s, d), BF16)],
            compiler_params=_params(1), name="xattn_sample")(x_all, g_q, wq, cache_mem_k, cache_mem_v, wo)

        g_f = row(norm_ffn[i])
        c = i // 2
        if i % 2 == 0:
            tf = _pick(d_ff, (1408, 1024, 512, 256, 128))
            x_all = _ffn(x_all, g_f, ffn_w_gu[c].astype(BF16), ffn_w_down[c].astype(BF16), tm=tm, tf=tf)
        else:
            router = jnp.concatenate([moe_router[c], jnp.zeros((d, LANES - n_experts), F32)], axis=1)
            r_hi = router.astype(BF16)
            r_lo = (router - r_hi.astype(F32)).astype(BF16)
            idx, gates, rank, counts = _moe_router(x_all, g_f, r_hi, r_lo, tm=tm, n_experts=n_experts)
            r_max = (TOP_K * m + n_experts * (tm - 1)) // tm
            dest, n_tiles, tile_expert, tile_end = _moe_plan(
                idx[:, :TOP_K], rank[:, :TOP_K], counts[0, :n_experts].astype(jnp.int32), tm, r_max)
            hn_sorted = _moe_dispatch(x_all, g_f, dest, tile_end, tm=tm, rows_sorted=r_max * tm)
            tf = _pick(d_exp, (1792, 1024, 512, 256, 128))
            y_sorted = _moe_ffn(hn_sorted, tile_expert, n_tiles, moe_w_gu_b, moe_w_down_b, layer=c, tm=tm, tf=tf)
            last = i == depth - 1
            x_all = _moe_combine(x_all, gates, dest, y_sorted, tm=tm,
                                 final_gain=row(norm_final) if last else None)

    if depth % 2 == 0:
        y_all = x_all
    else:
        (y_all,) = _row_call(_final_norm_body, grid_tiles=n_t, tm=tm, first_tile=0, row_ins=[x_all],
                             full_ins=[row(norm_final)], outs=[(m, d, F32, 0)], alias_x=False,
                             name="final_norm")
    y_prompt = y_all[:t_p][None]
    y_sample = y_all[t_p:].reshape(n_b, s_new, d)
    mem_shape = (depth, 1, n_mem, xa_heads, d // xa_heads)
    fk_p, fv_p, fl_p, fk_s, fv_s, fl_s = zip(*fox_out)
    hshape_p = (1, t_p, fox_heads, d // fox_heads)
    hshape_s = (n_b, s_new, fox_heads, d // fox_heads)
    return (y_prompt, y_sample, mem_k.reshape(mem_shape), mem_v.reshape(mem_shape),
            jnp.stack(conv_p), jnp.stack(conv_s),
            jnp.stack([a.reshape(hshape_p) for a in fk_p]), jnp.stack([a.reshape(hshape_p) for a in fv_p]),
            jnp.stack([a.reshape(1, t_p, fox_heads) for a in fl_p]),
            jnp.stack([a.reshape(hshape_s) for a in fk_s]), jnp.stack([a.reshape(hshape_s) for a in fv_s]),
            jnp.stack([a.reshape(n_b, s_new, fox_heads) for a in fl_s]),
            jnp.stack(sgu_v))
```
